```python
import math
import jax, jax.numpy as jnp
from jax import lax
import numpy as np

D_MODEL = 4096
BATCH = 4
SEQ = 2048
DEPTH = 4
DEC_BATCH = 8
DEC_SEQ = 1
PAST_LEN = 8192
PAGE_SIZE = 128

N_MIXERS = 2
N_ATTN_LAYERS = (DEPTH + 1) // 2
N_CONV_LAYERS = DEPTH // 2
DIFF_HEAD_DIM = 128
DIFF_HEADS = D_MODEL // (2 * DIFF_HEAD_DIM)
DIFF_V_DIM = 2 * DIFF_HEAD_DIM
ROT_DIM = DIFF_HEAD_DIM // 4
ROPE_THETA = 500000.0
Q_BLOCK = 128
CONV_WIDTH = 31
D_FF = ((8 * D_MODEL // 3 + 255) // 256) * 256
MEM_TOKENS = 256
MEM_HEADS = 4
MEM_HEAD_DIM = 128
RMS_EPS = 1e-6
SUBLN_EPS = 1e-5
LN_EPS = 1e-5

kernel_name = "hybrid_diffattn_conformer_macaron_decode_step"


def rms_norm(x, g, eps=RMS_EPS):
    xf = x.astype(jnp.float32)
    y = xf * lax.rsqrt(jnp.mean(xf * xf, axis=-1, keepdims=True) + eps)
    return (y * g.astype(jnp.float32)).astype(x.dtype)


def layer_norm(x, g, b, eps=LN_EPS):
    xf = x.astype(jnp.float32)
    mu = jnp.mean(xf, axis=-1, keepdims=True)
    var = jnp.mean(jnp.square(xf - mu), axis=-1, keepdims=True)
    y = (xf - mu) * lax.rsqrt(var + eps)
    return (y * g.astype(jnp.float32) + b.astype(jnp.float32)).astype(x.dtype)


def swiglu(x, w_in, w_out):
    g, u = jnp.split(x @ w_in, 2, axis=-1)
    return (jax.nn.silu(g) * u) @ w_out


def partial_rope(x, pos):
    half = ROT_DIM // 2
    inv_freq = jnp.power(jnp.float32(ROPE_THETA), -(jnp.arange(0, ROT_DIM, 2, dtype=jnp.float32) / ROT_DIM))
    ang = pos.astype(jnp.float32)[:, None] * inv_freq[None, :]
    cos = jnp.cos(ang)[None, :, None, :]
    sin = jnp.sin(ang)[None, :, None, :]
    xr = x[..., :ROT_DIM].astype(jnp.float32)
    x1, x2 = xr[..., :half], xr[..., half:]
    rot = jnp.concatenate([x1 * cos - x2 * sin, x2 * cos + x1 * sin], axis=-1).astype(x.dtype)
    return jnp.concatenate([rot, x[..., ROT_DIM:]], axis=-1)


def diff_qkv(h, w_qkv, pos):
    B, S, _ = h.shape
    q, k, v = jnp.split(h @ w_qkv, 3, axis=-1)
    q = partial_rope(q.reshape(B, S, 2 * DIFF_HEADS, DIFF_HEAD_DIM), pos)
    k = partial_rope(k.reshape(B, S, 2 * DIFF_HEADS, DIFF_HEAD_DIM), pos)
    v = v.reshape(B, S, DIFF_HEADS, DIFF_V_DIM)
    return q, k, v


def diff_lambda(lam, lambda_init):
    lf = lam.astype(jnp.float32)
    return jnp.exp(jnp.sum(lf[0] * lf[1])) - jnp.exp(jnp.sum(lf[2] * lf[3])) + lambda_init


def diff_attention(q, k, v, q_pos, k_pos, lam_full, lambda_init, subln_g):
    B, Sq = q.shape[0], q.shape[1]
    Sk = k.shape[1]
    scale = DIFF_HEAD_DIM ** -0.5

    def block(args):
        qb, pb = args
        T = qb.shape[1]
        s = jnp.einsum('bqcd,bkcd->bcqk', qb, k).astype(jnp.float32) * scale
        mask = k_pos[None, :] <= pb[:, None]
        s = jnp.where(mask, s, -jnp.inf)
        p = jax.nn.softmax(s, axis=-1).reshape(B, DIFF_HEADS, 2, T, Sk)
        a = p[:, :, 0] - lam_full * p[:, :, 1]
        return jnp.einsum('bhqk,bkhe->bqhe', a.astype(v.dtype), v)

    if Sq > Q_BLOCK and Sq % Q_BLOCK == 0:
        nb = Sq // Q_BLOCK
        qb = q.reshape(B, nb, Q_BLOCK, 2 * DIFF_HEADS, DIFF_HEAD_DIM).transpose(1, 0, 2, 3, 4)
        pb = q_pos.reshape(nb, Q_BLOCK)
        o = lax.map(block, (qb, pb))
        o = o.transpose(1, 0, 2, 3, 4).reshape(B, Sq, DIFF_HEADS, DIFF_V_DIM)
    else:
        o = block((q, q_pos))
    o = rms_norm(o, subln_g, SUBLN_EPS) * (1.0 - lambda_init)
    return o.reshape(B, Sq, DIFF_HEADS * DIFF_V_DIM)


def conv_module(h, hist, w_in, b_in, w_dw, b_dw, ln_g, ln_b, w_out, b_out):
    a, gate = jnp.split(h @ w_in + b_in, 2, axis=-1)
    u = a * jax.nn.sigmoid(gate)
    full = jnp.concatenate([hist.astype(u.dtype), u], axis=1)
    c = lax.conv_general_dilated(full, w_dw[:, None, :].astype(u.dtype), window_strides=(1,),
                                 padding='VALID', dimension_numbers=('NWC', 'WIO', 'NWC'),
                                 feature_group_count=D_MODEL) + b_dw
    c = layer_norm(c, ln_g, ln_b)
    y = jax.nn.silu(c) @ w_out + b_out
    return y, full[:, -(CONV_WIDTH - 1):]


def mem_kv(mem, g, w_kv):
    B, M, _ = mem.shape
    k, v = jnp.split(rms_norm(mem, g) @ w_kv, 2, axis=-1)
    return (k.reshape(B, M, MEM_HEADS, MEM_HEAD_DIM), v.reshape(B, M, MEM_HEADS, MEM_HEAD_DIM))


def cross_attend(h, mk, mv, w_q, w_o):
    B, S, _ = h.shape
    q = (h @ w_q).reshape(B, S, MEM_HEADS, MEM_HEAD_DIM)
    s = jnp.einsum('bshd,bmhd->bhsm', q, mk).astype(jnp.float32) * (MEM_HEAD_DIM ** -0.5)
    p = jax.nn.softmax(s, axis=-1).astype(mv.dtype)
    o = jnp.einsum('bhsm,bmhd->bshd', p, mv).reshape(B, S, MEM_HEADS * MEM_HEAD_DIM)
    return o @ w_o


def setup_inputs(seed: int = 0) -> dict:
    key = jax.random.key(seed)
    ks = iter(jax.random.split(key, 40))

    def nrm(shape, scale=1.0):
        return jax.random.normal(next(ks), shape, jnp.float32) * scale

    def gain(shape):
        return 1.0 + nrm(shape, 0.02)

    d = D_MODEL
    n_pages = PAST_LEN // PAGE_SIZE
    n_used = DEC_BATCH * n_pages
    n_pool = n_used + max(1, n_used // 4)
    inp = {}
    inp["x_prompt"] = nrm((BATCH, SEQ, d))
    inp["x_sample"] = nrm((DEC_BATCH, DEC_SEQ, d))
    inp["cache_k"] = nrm((n_pool, N_ATTN_LAYERS, PAGE_SIZE, 2 * DIFF_HEADS, DIFF_HEAD_DIM))
    inp["cache_v"] = nrm((n_pool, N_ATTN_LAYERS, PAGE_SIZE, DIFF_HEADS, DIFF_V_DIM))
    inp["state_conv"] = nrm((DEC_BATCH, N_CONV_LAYERS, CONV_WIDTH - 1, d), 0.5)
    inp["cache_mem_k"] = nrm((DEC_BATCH, DEPTH, MEM_TOKENS, MEM_HEADS, MEM_HEAD_DIM))
    inp["cache_mem_v"] = nrm((DEC_BATCH, DEPTH, MEM_TOKENS, MEM_HEADS, MEM_HEAD_DIM))
    inp["page_table"] = jax.random.permutation(next(ks), n_pool)[:n_used].reshape(DEC_BATCH, n_pages).astype(jnp.int32)
    inp["mem_prompt"] = nrm((BATCH, MEM_TOKENS, d))
    inp["ln_ffn1"] = gain((DEPTH, d))
    inp["ffn1_w_in"] = nrm((DEPTH, d, 2 * D_FF), d ** -0.5)
    inp["ffn1_w_out"] = nrm((DEPTH, D_FF, d), D_FF ** -0.5)
    inp["ln_mix"] = gain((DEPTH, d))
    inp["attn_w_qkv"] = nrm((N_ATTN_LAYERS, d, 3 * d), d ** -0.5)
    inp["attn_w_o"] = nrm((N_ATTN_LAYERS, d, d), d ** -0.5)
    inp["attn_lambda"] = nrm((N_ATTN_LAYERS, 4, DIFF_HEAD_DIM), 0.1)
    inp["attn_subln"] = gain((N_ATTN_LAYERS, DIFF_V_DIM))
    inp["conv_w_in"] = nrm((N_CONV_LAYERS, d, 2 * d), d ** -0.5)
    inp["conv_b_in"] = nrm((N_CONV_LAYERS, 2 * d), 0.02)
    inp["conv_w_dw"] = nrm((N_CONV_LAYERS, CONV_WIDTH, d), CONV_WIDTH ** -0.5)
    inp["conv_b_dw"] = nrm((N_CONV_LAYERS, d), 0.02)
    inp["conv_ln_g"] = gain((N_CONV_LAYERS, d))
    inp["conv_ln_b"] = nrm((N_CONV_LAYERS, d), 0.02)
    inp["conv_w_out"] = nrm((N_CONV_LAYERS, d, d), d ** -0.5)
    inp["conv_b_out"] = nrm((N_CONV_LAYERS, d), 0.02)
    inp["ln_cross"] = gain((DEPTH, d))
    inp["ln_mem"] = gain((DEPTH, d))
    inp["cross_w_q"] = nrm((DEPTH, d, MEM_HEADS * MEM_HEAD_DIM), d ** -0.5)
    inp["cross_w_kv"] = nrm((DEPTH, d, 2 * MEM_HEADS * MEM_HEAD_DIM), d ** -0.5)
    inp["cross_w_o"] = nrm((DEPTH, MEM_HEADS * MEM_HEAD_DIM, d), (MEM_HEADS * MEM_HEAD_DIM) ** -0.5)
    inp["ln_ffn2"] = gain((DEPTH, d))
    inp["ffn2_w_in"] = nrm((DEPTH, d, 2 * D_FF), d ** -0.5)
    inp["ffn2_w_out"] = nrm((DEPTH, D_FF, d), D_FF ** -0.5)
    inp["ln_final"] = gain((d,))
    return inp


def reference(x_prompt, x_sample, cache_k, cache_v, state_conv, cache_mem_k, cache_mem_v, page_table,
              mem_prompt, ln_ffn1, ffn1_w_in, ffn1_w_out, ln_mix, attn_w_qkv, attn_w_o, attn_lambda,
              attn_subln, conv_w_in, conv_b_in, conv_w_dw, conv_b_dw, conv_ln_g, conv_ln_b, conv_w_out,
              conv_b_out, ln_cross, ln_mem, cross_w_q, cross_w_kv, cross_w_o, ln_ffn2, ffn2_w_in,
              ffn2_w_out, ln_final):
    n_pages = PAST_LEN // PAGE_SIZE
    pos_p = jnp.arange(SEQ, dtype=jnp.int32)
    pos_s = PAST_LEN + jnp.arange(DEC_SEQ, dtype=jnp.int32)
    kpos_s = jnp.arange(PAST_LEN + DEC_SEQ, dtype=jnp.int32)
    hist_p = jnp.zeros((BATCH, CONV_WIDTH - 1, D_MODEL), x_prompt.dtype)

    xp, xs = x_prompt, x_sample
    kp_l, vp_l, ks_l, vs_l = [], [], [], []
    cp_l, cs_l, mkp_l, mvp_l = [], [], [], []
    for i in range(DEPTH):
        j = i // N_MIXERS
        xp = xp + 0.5 * swiglu(rms_norm(xp, ln_ffn1[i]), ffn1_w_in[i], ffn1_w_out[i])
        xs = xs + 0.5 * swiglu(rms_norm(xs, ln_ffn1[i]), ffn1_w_in[i], ffn1_w_out[i])
        hp = rms_norm(xp, ln_mix[i])
        hs = rms_norm(xs, ln_mix[i])
        if i % N_MIXERS == 0:
            lam_init = 0.8 - 0.6 * math.exp(-0.3 * i)
            lam_full = diff_lambda(attn_lambda[j], lam_init)
            qp, kp, vp = diff_qkv(hp, attn_w_qkv[j], pos_p)
            qs, ks, vs = diff_qkv(hs, attn_w_qkv[j], pos_s)
            past_k = cache_k[page_table, j].reshape(DEC_BATCH, n_pages * PAGE_SIZE, 2 * DIFF_HEADS, DIFF_HEAD_DIM)
            past_v = cache_v[page_table, j].reshape(DEC_BATCH, n_pages * PAGE_SIZE, DIFF_HEADS, DIFF_V_DIM)
            k_all = jnp.concatenate([past_k.astype(ks.dtype), ks], axis=1)
            v_all = jnp.concatenate([past_v.astype(vs.dtype), vs], axis=1)
            op = diff_attention(qp, kp, vp, pos_p, pos_p, lam_full, lam_init, attn_subln[j])
            os_ = diff_attention(qs, k_all, v_all, pos_s, kpos_s, lam_full, lam_init, attn_subln[j])
            xp = xp + op @ attn_w_o[j]
            xs = xs + os_ @ attn_w_o[j]
            kp_l.append(kp); vp_l.append(vp); ks_l.append(ks); vs_l.append(vs)
        else:
            yp, new_cp = conv_module(hp, hist_p, conv_w_in[j], conv_b_in[j], conv_w_dw[j], conv_b_dw[j],
                                     conv_ln_g[j], conv_ln_b[j], conv_w_out[j], conv_b_out[j])
            ys, new_cs = conv_module(hs, state_conv[:, j], conv_w_in[j], conv_b_in[j], conv_w_dw[j], conv_b_dw[j],
                                     conv_ln_g[j], conv_ln_b[j], conv_w_out[j], conv_b_out[j])
            xp = xp + yp
            xs = xs + ys
            cp_l.append(new_cp); cs_l.append(new_cs)
        mkp, mvp = mem_kv(mem_prompt, ln_mem[i], cross_w_kv[i])
        xp = xp + cross_attend(rms_norm(xp, ln_cross[i]), mkp, mvp, cross_w_q[i], cross_w_o[i])
        xs = xs + cross_attend(rms_norm(xs, ln_cross[i]), cache_mem_k[:, i].astype(xs.dtype),
                               cache_mem_v[:, i].astype(xs.dtype), cross_w_q[i], cross_w_o[i])
        mkp_l.append(mkp); mvp_l.append(mvp)
        xp = xp + 0.5 * swiglu(rms_norm(xp, ln_ffn2[i]), ffn2_w_in[i], ffn2_w_out[i])
        xs = xs + 0.5 * swiglu(rms_norm(xs, ln_ffn2[i]), ffn2_w_in[i], ffn2_w_out[i])

    y_prompt = rms_norm(xp, ln_final)
    y_sample = rms_norm(xs, ln_final)
    new_k_prompt = jnp.stack(kp_l, axis=1)
    new_v_prompt = jnp.stack(vp_l, axis=1)
    new_k_sample = jnp.stack(ks_l, axis=1)
    new_v_sample = jnp.stack(vs_l, axis=1)
    new_conv_prompt = jnp.stack(cp_l, axis=1)
    new_conv_sample = jnp.stack(cs_l, axis=1)
    new_mem_k_prompt = jnp.stack(mkp_l, axis=1)
    new_mem_v_prompt = jnp.stack(mvp_l, axis=1)
    return (y_prompt, y_sample, new_k_prompt, new_v_prompt, new_k_sample, new_v_sample,
            new_conv_prompt, new_conv_sample, new_mem_k_prompt, new_mem_v_prompt)
```

```python
import functools
import math
from typing import NamedTuple

import jax
import jax.numpy as jnp
from jax import lax
from jax.experimental import pallas as pl
from jax.experimental.pallas import tpu as pltpu

F32 = jnp.float32
BF16 = jnp.bfloat16

V7X_LANES = 128
V7X_SUBLANES = 8
V7X_VMEM_LIMIT_CAP = 56 * 1024 * 1024

N_MIXERS = 2
CONV_WIDTH = 31
ROPE_THETA = 500000.0
RMS_EPS = 1e-6
SUBLN_EPS = 1e-5
LN_EPS = 1e-5


class Cfg(NamedTuple):
    d_model: int
    head_dim: int
    rot_dim: int
    mem_heads: int
    mem_head_dim: int
    page_size: int
    past_len: int
    tm: int
    tn: int
    tm_ffn_out: int
    tn_ffn_out: int
    tq: int
    t_conv: int
    t_xattn: int
    t_norm: int


def _nbytes(shape, dtype):
    return math.prod(shape) * jnp.dtype(dtype).itemsize


def _params(semantics, vmem_bytes):
    limit = min(V7X_VMEM_LIMIT_CAP, max(32 * 1024 * 1024, int(vmem_bytes)))
    return pltpu.CompilerParams(dimension_semantics=semantics, vmem_limit_bytes=limit)


def _rms(x, eps):
    return x * lax.rsqrt(jnp.mean(x * x, axis=-1, keepdims=True) + eps)


def _rmsnorm_body(x_ref, g_ref, o_ref):
    x = x_ref[...].astype(F32)
    o_ref[...] = (_rms(x, RMS_EPS) * g_ref[...]).astype(o_ref.dtype)


def _rmsnorm(x, g3, li, out_dtype, t_norm):
    m, d = x.shape
    tr = min(m, t_norm)
    return pl.pallas_call(
        _rmsnorm_body,
        grid=(m // tr,),
        in_specs=[pl.BlockSpec((tr, d), lambda i: (i, 0)),
                  pl.BlockSpec((None, 1, d), lambda i: (li, 0, 0))],
        out_specs=pl.BlockSpec((tr, d), lambda i: (i, 0)),
        out_shape=jax.ShapeDtypeStruct((m, d), out_dtype),
        compiler_params=_params(("parallel",), 6 * _nbytes((tr, d), F32)),
        name="rmsnorm",
    )(x, g3)


def _rope_lanes(y, c, s_lo, s_hi, rot_half):
    groups = []
    for g in range(y.shape[1] // V7X_LANES):
        yg = y[:, g * V7X_LANES:(g + 1) * V7X_LANES]
        groups.append(yg * c
                      + pltpu.roll(yg, V7X_LANES - rot_half, 1) * s_lo
                      + pltpu.roll(yg, rot_half, 1) * s_hi)
    return groups[0] if len(groups) == 1 else jnp.concatenate(groups, axis=1)


def _mm_body(*refs, n_w, has_bias, rot_half, res_scale, act):
    it = iter(refs)
    x_ref = next(it)
    w_refs = [next(it) for _ in range(n_w)]
    b_refs = [next(it) for _ in range(n_w)] if has_bias else []
    rope_refs = [next(it) for _ in range(3)] if rot_half else []
    res_ref = next(it) if res_scale is not None else None
    o_ref = next(it)

    x = x_ref[...]
    ys = []
    for t in range(n_w):
        y = jnp.dot(x, w_refs[t][...].astype(BF16), preferred_element_type=F32)
        if has_bias:
            y = y + b_refs[t][...]
        ys.append(y)
    if act == "swiglu":
        y = jax.nn.silu(ys[0]) * ys[1]
    elif act == "glu":
        y = ys[0] * jax.nn.sigmoid(ys[1])
    else:
        y = ys[0]
    if rot_half:
        y = _rope_lanes(y, rope_refs[0][...], rope_refs[1][...], rope_refs[2][...], rot_half)
    if res_scale is not None:
        y = res_ref[...] + res_scale * y
    o_ref[...] = y.astype(o_ref.dtype)


def _mm(x, w, li, *, col_offsets, n_cols, tm, tn, out_dtype, bias=None, rope=None,
        rot_half=0, res=None, res_scale=None, act=None, single_buffer_x=False):
    m, k = x.shape
    tm = min(tm, m)
    tn = min(tn, n_cols)
    n_w = len(col_offsets)
    grid = (m // tm, n_cols // tn)
    x_mode = dict(pipeline_mode=pl.Buffered(1)) if single_buffer_x else {}
    in_specs = [pl.BlockSpec((tm, k), lambda i, j: (i, 0), **x_mode)]
    args = [x]
    for off in col_offsets:
        in_specs.append(pl.BlockSpec((None, k, tn), lambda i, j, o=off // tn: (li, 0, j + o)))
        args.append(w)
    if bias is not None:
        for off in col_offsets:
            in_specs.append(pl.BlockSpec((None, 1, tn), lambda i, j, o=off // tn: (li, 0, j + o)))
            args.append(bias)
    if rope is not None:
        nblk = rope[0].shape[0] // tm
        for tab in rope:
            in_specs.append(pl.BlockSpec((tm, V7X_LANES), lambda i, j: (i % nblk, 0)))
            args.append(tab)
    if res is not None:
        in_specs.append(pl.BlockSpec((tm, tn), lambda i, j: (i, j)))
        args.append(res)
    vmem = ((1 if single_buffer_x else 2) * _nbytes((tm, k), BF16)
            + n_w * (2 * _nbytes((k, tn), F32) + _nbytes((k, tn), BF16))
            + (4 + 2 * n_w) * _nbytes((tm, tn), F32) + (8 << 20))
    body = functools.partial(_mm_body, n_w=n_w, has_bias=bias is not None,
                             rot_half=rot_half if rope is not None else 0,
                             res_scale=res_scale if res is not None else None, act=act)
    return pl.pallas_call(
        body,
        grid=grid,
        in_specs=in_specs,
        out_specs=pl.BlockSpec((tm, tn), lambda i, j: (i, j)),
        out_shape=jax.ShapeDtypeStruct((m, n_cols), out_dtype),
        compiler_params=_params(("parallel", "parallel"), vmem),
        name="matmul",
    )(*args)


def _diff_lambda(lam, lam_init):
    a = jnp.sum(lam[0:1] * lam[1:2], axis=-1, keepdims=True)
    b = jnp.sum(lam[2:3] * lam[3:4], axis=-1, keepdims=True)
    return jnp.exp(a) - jnp.exp(b) + lam_init


def _flash_body(q_ref, k_ref, v_ref, lam_ref, g_ref, o_ref, m_sc, l_sc, acc_sc, *,
                hd, lam_init):
    qi = pl.program_id(2)
    ki = pl.program_id(3)
    scale = hd ** -0.5

    @pl.when(ki == 0)
    def _init():
        m_sc[...] = jnp.full(m_sc.shape, -jnp.inf, F32)
        l_sc[...] = jnp.zeros(l_sc.shape, F32)
        acc_sc[...] = jnp.zeros(acc_sc.shape, F32)

    def step(masked):
        q = q_ref[0]
        k = k_ref[0].astype(BF16)
        v = v_ref[0].astype(BF16)
        for c in range(2):
            s = lax.dot_general(q[:, c * hd:(c + 1) * hd], k[:, c * hd:(c + 1) * hd],
                                (((1,), (1,)), ((), ())), preferred_element_type=F32) * scale
            if masked:
                row = lax.broadcasted_iota(jnp.int32, s.shape, 0)
                col = lax.broadcasted_iota(jnp.int32, s.shape, 1)
                s = jnp.where(col <= row, s, -jnp.inf)
            m_prev = m_sc[c]
            m_new = jnp.maximum(m_prev, jnp.max(s, axis=-1, keepdims=True))
            alpha = jnp.exp(m_prev - m_new)
            p = jnp.exp(s - m_new)
            l_sc[c] = alpha * l_sc[c] + jnp.sum(p, axis=-1, keepdims=True)
            acc_sc[c] = alpha * acc_sc[c] + jnp.dot(p.astype(BF16), v, preferred_element_type=F32)
            m_sc[c] = m_new

    @pl.when(ki < qi)
    def _off_diagonal():
        step(False)

    @pl.when(ki == qi)
    def _diagonal():
        step(True)
        lam = _diff_lambda(lam_ref[...], lam_init)
        o = acc_sc[0] / l_sc[0] - lam * (acc_sc[1] / l_sc[1])
        o = _rms(o, SUBLN_EPS) * g_ref[...] * (1.0 - lam_init)
        o_ref[0] = o.astype(o_ref.dtype)


def _flash_diff_attn(q, k, v, lam, g, li, *, cfg, lam_init):
    b, s, d = q.shape
    hd = cfg.head_dim
    hw = 2 * hd
    t = min(cfg.tq, s)
    nq = s // t
    body = functools.partial(_flash_body, hd=hd, lam_init=lam_init)
    return pl.pallas_call(
        body,
        grid=(b, d // hw, nq, nq),
        in_specs=[pl.BlockSpec((1, t, hw), lambda bi, h, qi, ki: (bi, qi, h)),
                  pl.BlockSpec((1, t, hw), lambda bi, h, qi, ki: (bi, jnp.minimum(ki, qi), h)),
                  pl.BlockSpec((1, t, hw), lambda bi, h, qi, ki: (bi, jnp.minimum(ki, qi), h)),
                  pl.BlockSpec((None, 4, hd), lambda bi, h, qi, ki: (li, 0, 0)),
                  pl.BlockSpec((None, 1, hw), lambda bi, h, qi, ki: (li, 0, 0))],
        out_specs=pl.BlockSpec((1, t, hw), lambda bi, h, qi, ki: (bi, qi, h)),
        out_shape=jax.ShapeDtypeStruct((b, s, d), BF16),
        scratch_shapes=[pltpu.VMEM((2, t, 1), F32), pltpu.VMEM((2, t, 1), F32),
                        pltpu.VMEM((2, t, hw), F32)],
        compiler_params=_params(("parallel", "parallel", "parallel", "arbitrary"), 32 << 20),
        name="flash_diff_attn",
    )(q, k, v, lam, g)


def _decode_attn_body(pt_ref, q_ref, kn_ref, vn_ref, ck_ref, cv_ref, lam_ref, g_ref, o_ref,
                      qbd_sc, m_sc, l_sc, acc_sc, *, hd, lam_init, n_pages):
    del pt_ref
    p_idx = pl.program_id(1)
    scale = hd ** -0.5
    nc, d = acc_sc.shape
    hd_shift = hd.bit_length() - 1

    @pl.when(p_idx == 0)
    def _init():
        qb = q_ref[0].astype(BF16)
        kb = kn_ref[0].astype(BF16).astype(F32)
        m_sc[...] = jnp.sum(qb.astype(F32) * kb, axis=-1, keepdims=True) * scale
        l_sc[...] = jnp.ones(l_sc.shape, F32)
        acc_sc[...] = jnp.broadcast_to(vn_ref[...].astype(BF16).astype(F32), acc_sc.shape)
        qt = jnp.concatenate([qb] * nc, axis=1)
        row = lax.broadcasted_iota(jnp.int32, (nc, d), 0)
        col = lax.broadcasted_iota(jnp.int32, (nc, d), 1)
        qbd_sc[...] = jnp.where((col >> hd_shift) == row, qt, jnp.zeros_like(qt))

    k2 = ck_ref[0, 0].astype(BF16)
    s = lax.dot_general(qbd_sc[...], k2, (((1,), (1,)), ((), ())),
                        preferred_element_type=F32) * scale
    m_prev = m_sc[...]
    m_new = jnp.maximum(m_prev, jnp.max(s, axis=-1, keepdims=True))
    alpha = jnp.exp(m_prev - m_new)
    p = jnp.exp(s - m_new)
    l_sc[...] = alpha * l_sc[...] + jnp.sum(p, axis=-1, keepdims=True)
    acc_sc[...] = alpha * acc_sc[...] + jnp.dot(p.astype(BF16), cv_ref[0, 0].astype(BF16),
                                                preferred_element_type=F32)
    m_sc[...] = m_new

    @pl.when(p_idx == n_pages - 1)
    def _finish():
        lam = _diff_lambda(lam_ref[...], lam_init)
        hw_shift = hd_shift + 1
        nh = nc // 2
        row = lax.broadcasted_iota(jnp.int32, (nc, d), 0)
        col = lax.broadcasted_iota(jnp.int32, (nc, d), 1)
        coef = jnp.where((row & 1) == 0, jnp.ones((nc, d), F32), -lam)
        own = (col >> hw_shift) == (row >> 1)
        o = jnp.sum(jnp.where(own, (acc_sc[...] / l_sc[...]) * coef, 0.0), axis=0, keepdims=True)
        grp = (lax.broadcasted_iota(jnp.int32, (nh, d), 1) >> hw_shift
               == lax.broadcasted_iota(jnp.int32, (nh, d), 0))
        ssq = jnp.sum(jnp.where(grp, o * o, 0.0), axis=1, keepdims=True)
        rs = lax.rsqrt(ssq / (2 * hd) + SUBLN_EPS)
        rs_row = jnp.sum(jnp.where(grp, rs, 0.0), axis=0, keepdims=True)
        o_ref[...] = (o * rs_row * g_ref[...] * (1.0 - lam_init)).astype(o_ref.dtype)


def _decode_diff_attn(q, k_new, v_new, cache_k, cache_v, page_table, lam, g_row, li, *,
                      cfg, lam_init):
    bd, nc, hd = q.shape
    d = nc * hd
    n_pages = page_table.shape[1]
    page = cfg.page_size
    body = functools.partial(_decode_attn_body, hd=hd, lam_init=lam_init, n_pages=n_pages)
    grid_spec = pltpu.PrefetchScalarGridSpec(
        num_scalar_prefetch=1,
        grid=(bd, n_pages),
        in_specs=[pl.BlockSpec((1, nc, hd), lambda b, p, pt: (b, 0, 0)),
                  pl.BlockSpec((1, nc, hd), lambda b, p, pt: (b, 0, 0)),
                  pl.BlockSpec((None, 1, d), lambda b, p, pt: (b, 0, 0)),
                  pl.BlockSpec((1, 1, page, d), lambda b, p, pt: (pt[b * n_pages + p], li, 0, 0)),
                  pl.BlockSpec((1, 1, page, d), lambda b, p, pt: (pt[b * n_pages + p], li, 0, 0)),
                  pl.BlockSpec((None, 4, hd), lambda b, p, pt: (li, 0, 0)),
                  pl.BlockSpec((None, 1, d), lambda b, p, pt: (li, 0, 0))],
        out_specs=pl.BlockSpec((None, 1, d), lambda b, p, pt: (b, 0, 0)),
        scratch_shapes=[pltpu.VMEM((nc, d), BF16), pltpu.VMEM((nc, 1), F32),
                        pltpu.VMEM((nc, 1), F32), pltpu.VMEM((nc, d), F32)],
    )
    return pl.pallas_call(
        body,
        grid_spec=grid_spec,
        out_shape=jax.ShapeDtypeStruct((bd, 1, d), F32),
        compiler_params=_params(("parallel", "arbitrary"),
                                6 * _nbytes((page, d), F32) + (8 << 20)),
        name="decode_diff_attn",
    )(page_table.reshape(-1), q, k_new, v_new, cache_k, cache_v, lam, g_row)


def _ln_silu(c, g, b):
    mu = jnp.mean(c, axis=-1, keepdims=True)
    var = jnp.mean(jnp.square(c - mu), axis=-1, keepdims=True)
    y = (c - mu) * lax.rsqrt(var + LN_EPS) * g + b
    return jax.nn.silu(y)


def _conv_body(cur_ref, prev_ref, wdw_ref, bdw_ref, g_ref, b_ref, o_ref, ext_sc, c_sc, *,
               ts, halo, chunk):
    si = pl.program_id(1)
    ext_sc[0:halo, :] = jnp.where(si > 0, prev_ref[0], 0.0)
    ext_sc[halo:halo + ts, :] = cur_ref[0]
    lead = halo - (CONV_WIDTH - 1)
    d = cur_ref.shape[2]
    for c0 in range(0, d, chunk):
        acc = jnp.zeros((ts, chunk), F32)
        for w in range(CONV_WIDTH):
            acc = acc + ext_sc[lead + w:lead + w + ts, c0:c0 + chunk] * wdw_ref[w:w + 1, c0:c0 + chunk]
        c_sc[:, c0:c0 + chunk] = acc + bdw_ref[:, c0:c0 + chunk]
    o_ref[0] = _ln_silu(c_sc[...], g_ref[...], b_ref[...]).astype(o_ref.dtype)


def _conv_prompt(u, w_dw, b_dw, ln_g, ln_b, li, *, cfg):
    b, s, d = u.shape
    ts = min(cfg.t_conv, s)
    halo = 32
    per = ts // halo
    body = functools.partial(_conv_body, ts=ts, halo=halo, chunk=min(d, 512))
    vec = pl.BlockSpec((None, 1, d), lambda bi, si: (li, 0, 0))
    return pl.pallas_call(
        body,
        grid=(b, s // ts),
        in_specs=[pl.BlockSpec((1, ts, d), lambda bi, si: (bi, si, 0)),
                  pl.BlockSpec((1, halo, d), lambda bi, si: (bi, jnp.maximum(si * per - 1, 0), 0)),
                  pl.BlockSpec((None, CONV_WIDTH, d), lambda bi, si: (li, 0, 0)),
                  vec, vec, vec],
        out_specs=pl.BlockSpec((1, ts, d), lambda bi, si: (bi, si, 0)),
        out_shape=jax.ShapeDtypeStruct((b, s, d), BF16),
        scratch_shapes=[pltpu.VMEM((halo + ts, d), F32), pltpu.VMEM((ts, d), F32)],
        compiler_params=_params(("parallel", "arbitrary"), 32 << 20),
        name="conv_prompt",
    )(u, u, w_dw, b_dw, ln_g, ln_b)


def _conv_decode_body(hist_ref, u_ref, wdw_ref, bdw_ref, g_ref, b_ref, o_ref):
    nb = hist_ref.shape[0]
    k = CONV_WIDTH - 1
    w_hist = wdw_ref[0:k, :]
    rows = [jnp.sum(hist_ref[bi] * w_hist, axis=0, keepdims=True) for bi in range(nb)]
    c = jnp.concatenate(rows, axis=0) + u_ref[...] * wdw_ref[k:k + 1, :] + bdw_ref[...]
    o_ref[...] = _ln_silu(c, g_ref[...], b_ref[...]).astype(o_ref.dtype)


def _conv_decode(state_conv, u, w_dw, b_dw, ln_g, ln_b, li):
    bd, _, k, d = state_conv.shape
    vec = pl.BlockSpec((None, 1, d), lambda i: (li, 0, 0))
    return pl.pallas_call(
        _conv_decode_body,
        grid=(1,),
        in_specs=[pl.BlockSpec((bd, None, k, d), lambda i: (0, li, 0, 0)),
                  pl.BlockSpec((bd, d), lambda i: (0, 0)),
                  pl.BlockSpec((None, CONV_WIDTH, d), lambda i: (li, 0, 0)),
                  vec, vec, vec],
        out_specs=pl.BlockSpec((bd, d), lambda i: (0, 0)),
        out_shape=jax.ShapeDtypeStruct((bd, d), BF16),
        compiler_params=_params(("arbitrary",), 32 << 20),
        name="conv_decode",
    )(state_conv, u, w_dw, b_dw, ln_g, ln_b)


def _xattn_body(x_ref, g_ref, wq_ref, mk_ref, mv_ref, wo_ref, o_ref, *, n_heads, hd, pick_row):
    x = x_ref[...]
    h = (_rms(x, RMS_EPS) * g_ref[...]).astype(BF16)
    q = jnp.dot(h, wq_ref[...], preferred_element_type=F32).astype(BF16)
    mk = mk_ref[0, 0].astype(BF16)
    mv = mv_ref[0, 0].astype(BF16)
    scale = hd ** -0.5
    heads = []
    for hh in range(n_heads):
        sl = slice(hh * hd, (hh + 1) * hd)
        s = lax.dot_general(q[:, sl], mk[:, sl], (((1,), (1,)), ((), ())),
                            preferred_element_type=F32) * scale
        e = jnp.exp(s - jnp.max(s, axis=-1, keepdims=True))
        p = e / jnp.sum(e, axis=-1, keepdims=True)
        heads.append(jnp.dot(p.astype(BF16), mv[:, sl], preferred_element_type=F32))
    o = jnp.concatenate(heads, axis=1).astype(BF16)
    y = x + jnp.dot(o, wo_ref[...], preferred_element_type=F32)
    if pick_row:
        row = lax.broadcasted_iota(jnp.int32, y.shape, 0)
        y = jnp.sum(jnp.where(row == pl.program_id(0), y, 0.0), axis=0, keepdims=True)
    o_ref[...] = y


def _cross_attn(x, g3, wq, mk, mv, wo, li, *, cfg, rows_per_mem):
    m, d = x.shape
    n_heads, hd = cfg.mem_heads, cfg.mem_head_dim
    e = n_heads * hd
    t_mem = mk.shape[2]
    pick_row = rows_per_mem == 0
    if pick_row:
        tm, steps = m, m
        x_spec = pl.BlockSpec((m, d), lambda i: (0, 0))
        mem_idx = lambda i: (i, li, 0, 0)
        out_spec = pl.BlockSpec((None, 1, d), lambda i: (i, 0, 0))
        out_shape = jax.ShapeDtypeStruct((m, 1, d), F32)
    else:
        tm = min(cfg.t_xattn, rows_per_mem)
        steps = m // tm
        per = rows_per_mem // tm
        x_spec = pl.BlockSpec((tm, d), lambda i: (i, 0))
        mem_idx = lambda i: (i // per, li, 0, 0)
        out_spec = pl.BlockSpec((tm, d), lambda i: (i, 0))
        out_shape = jax.ShapeDtypeStruct((m, d), F32)
    body = functools.partial(_xattn_body, n_heads=n_heads, hd=hd, pick_row=pick_row)
    out = pl.pallas_call(
        body,
        grid=(steps,),
        in_specs=[x_spec,
                  pl.BlockSpec((None, 1, d), lambda i: (li, 0, 0)),
                  pl.BlockSpec((None, d, e), lambda i: (li, 0, 0)),
                  pl.BlockSpec((1, 1, t_mem, e), mem_idx),
                  pl.BlockSpec((1, 1, t_mem, e), mem_idx),
                  pl.BlockSpec((None, e, d), lambda i: (li, 0, 0))],
        out_specs=out_spec,
        out_shape=out_shape,
        compiler_params=_params(("parallel",), 8 * _nbytes((tm, d), F32)
                                + 4 * _nbytes((d, e), BF16) + (8 << 20)),
        name="cross_attn",
    )(x, g3, wq, mk, mv, wo)
    return out.reshape(m, d)


def _memkv_body(mem_ref, g_ref, wk_ref, wv_ref, ok_ref, ov_ref):
    h = (_rms(mem_ref[0], RMS_EPS) * g_ref[...]).astype(BF16)
    ok_ref[0, 0] = jnp.dot(h, wk_ref[...].astype(BF16), preferred_element_type=F32)
    ov_ref[0, 0] = jnp.dot(h, wv_ref[...].astype(BF16), preferred_element_type=F32)


def _mem_kv(mem, g3, w_kv, *, cfg):
    b, t, d = mem.shape
    n_layers = w_kv.shape[0]
    e = w_kv.shape[2] // 2
    tn = min(e, 256)
    nj = e // tn
    out = jax.ShapeDtypeStruct((b, n_layers, t, e), F32)
    out_spec = pl.BlockSpec((1, 1, t, tn), lambda l, j, bi: (bi, l, 0, j))
    return pl.pallas_call(
        _memkv_body,
        grid=(n_layers, nj, b),
        in_specs=[pl.BlockSpec((1, t, d), lambda l, j, bi: (bi, 0, 0)),
                  pl.BlockSpec((None, 1, d), lambda l, j, bi: (l, 0, 0)),
                  pl.BlockSpec((None, d, tn), lambda l, j, bi: (l, 0, j)),
                  pl.BlockSpec((None, d, tn), lambda l, j, bi: (l, 0, nj + j))],
        out_specs=[out_spec, out_spec],
        out_shape=[out, out],
        compiler_params=_params(("parallel", "parallel", "parallel"),
                                6 * _nbytes((d, tn), F32) + 4 * _nbytes((t, d), F32) + (8 << 20)),
        name="mem_kv",
    )(mem, g3, w_kv, w_kv)


def _rope_tables(pos, cfg):
    half = cfg.rot_dim // 2
    inv_freq = jnp.power(jnp.float32(ROPE_THETA),
                         -(jnp.arange(0, cfg.rot_dim, 2, dtype=F32) / cfg.rot_dim))
    ang = pos.astype(F32)[:, None] * inv_freq[None, :]
    cos, sin = jnp.cos(ang), jnp.sin(ang)
    pad = V7X_LANES - cfg.rot_dim
    n = pos.shape[0]
    c = jnp.concatenate([cos, cos, jnp.ones((n, pad), F32)], axis=1)
    s_lo = jnp.concatenate([-sin, jnp.zeros((n, half + pad), F32)], axis=1)
    s_hi = jnp.concatenate([jnp.zeros((n, half), F32), sin, jnp.zeros((n, pad), F32)], axis=1)
    return c, s_lo, s_hi


def _forward(cfg, x_prompt, x_sample, cache_k, cache_v, state_conv, cache_mem_k, cache_mem_v,
             page_table, mem_prompt, ln_ffn1, ffn1_w_in, ffn1_w_out, ln_mix, attn_w_qkv, attn_w_o,
             attn_lambda, attn_subln, conv_w_in, conv_b_in, conv_w_dw, conv_b_dw, conv_ln_g,
             conv_ln_b, conv_w_out, conv_b_out, ln_cross, ln_mem, cross_w_q, cross_w_kv, cross_w_o,
             ln_ffn2, ffn2_w_in, ffn2_w_out, ln_final):
    b, s, d = x_prompt.shape
    bd = x_sample.shape[0]
    depth = ln_ffn1.shape[0]
    d_ff = ffn1_w_out.shape[1]
    hd = cfg.head_dim
    hw = 2 * hd
    nc = d // hd
    n_pool, n_attn = cache_k.shape[0], cache_k.shape[1]
    e = cfg.mem_heads * cfg.mem_head_dim
    rot_half = cfg.rot_dim // 2

    row3 = lambda a: a.reshape(a.shape[0], 1, a.shape[-1])
    ln_ffn1, ln_mix, ln_cross, ln_mem, ln_ffn2 = map(row3, (ln_ffn1, ln_mix, ln_cross, ln_mem, ln_ffn2))
    conv_b_in, conv_b_dw, conv_ln_g, conv_ln_b, conv_b_out = map(
        row3, (conv_b_in, conv_b_dw, conv_ln_g, conv_ln_b, conv_b_out))
    subln = row3(attn_subln)
    subln_row = row3(jnp.tile(attn_subln, (1, d // hw)))
    wq_bf = cross_w_q.astype(BF16)
    wo_bf = cross_w_o.astype(BF16)
    cache_k2 = cache_k.reshape(n_pool, n_attn, cfg.page_size, d)
    cache_v2 = cache_v.reshape(n_pool, n_attn, cfg.page_size, d)
    mem_k_dec = cache_mem_k.reshape(bd, depth, -1, e)
    mem_v_dec = cache_mem_v.reshape(bd, depth, -1, e)

    rope_p = _rope_tables(jnp.arange(s, dtype=jnp.int32), cfg)
    rope_s = _rope_tables(jnp.full((bd,), cfg.past_len, jnp.int32), cfg)

    mem_k, mem_v = _mem_kv(mem_prompt, ln_mem, cross_w_kv, cfg=cfg)

    def ffn(x, g3, w_in, w_out, li):
        h = _rmsnorm(x, g3, li, BF16, cfg.t_norm)
        a = _mm(h, w_in, li, col_offsets=(0, d_ff), n_cols=d_ff, tm=cfg.tm, tn=cfg.tn // 2,
                out_dtype=BF16, act="swiglu")
        return _mm(a, w_out, li, col_offsets=(0,), n_cols=d, tm=cfg.tm_ffn_out, tn=cfg.tn_ffn_out,
                   out_dtype=F32, res=x, res_scale=0.5, single_buffer_x=True)

    def qkv(h, li, rope, q_dtype):
        mm = functools.partial(_mm, h, attn_w_qkv, li, n_cols=d, tm=cfg.tm, tn=cfg.tn)
        q = mm(col_offsets=(0,), out_dtype=q_dtype, rope=rope, rot_half=rot_half)
        k = mm(col_offsets=(d,), out_dtype=F32, rope=rope, rot_half=rot_half)
        v = mm(col_offsets=(2 * d,), out_dtype=F32)
        return q, k, v

    xp = x_prompt.reshape(b * s, d)
    xs = x_sample.reshape(bd, d)
    kp_l, vp_l, ks_l, vs_l, cp_l, cs_l = [], [], [], [], [], []
    for i in range(depth):
        j = i // N_MIXERS
        xp = ffn(xp, ln_ffn1, ffn1_w_in, ffn1_w_out, i)
        xs = ffn(xs, ln_ffn1, ffn1_w_in, ffn1_w_out, i)
        hp = _rmsnorm(xp, ln_mix, i, BF16, cfg.t_norm)
        hs = _rmsnorm(xs, ln_mix, i, BF16, cfg.t_norm)
        if i % N_MIXERS == 0:
            lam_init = 0.8 - 0.6 * math.exp(-0.3 * i)
            qp, kp, vp = qkv(hp, j, rope_p, BF16)
            qs, ks, vs = qkv(hs, j, rope_s, F32)
            op = _flash_diff_attn(qp.reshape(b, s, d), kp.reshape(b, s, d),
                                  vp.reshape(b, s, d), attn_lambda, subln, j,
                                  cfg=cfg, lam_init=lam_init)
            os_ = _decode_diff_attn(qs.reshape(bd, nc, hd), ks.reshape(bd, nc, hd),
                                    vs.reshape(bd, 1, d), cache_k2, cache_v2, page_table,
                                    attn_lambda, subln_row, j, cfg=cfg, lam_init=lam_init)
            xp = _mm(op.reshape(b * s, d), attn_w_o, j, col_offsets=(0,), n_cols=d, tm=cfg.tm,
                     tn=cfg.tn, out_dtype=F32, res=xp, res_scale=1.0)
            xs = _mm(os_.reshape(bd, d).astype(BF16), attn_w_o, j, col_offsets=(0,), n_cols=d, tm=cfg.tm,
                     tn=cfg.tn, out_dtype=F32, res=xs, res_scale=1.0)
            kp_l.append(kp.reshape(b, s, nc, hd))
            vp_l.append(vp.reshape(b, s, nc // 2, hw))
            ks_l.append(ks.reshape(bd, 1, nc, hd))
            vs_l.append(vs.reshape(bd, 1, nc // 2, hw))
        else:
            glu = functools.partial(_mm, w=conv_w_in, li=j, col_offsets=(0, d), n_cols=d, tm=cfg.tm,
                                    tn=cfg.tn // 2, out_dtype=F32, bias=conv_b_in, act="glu")
            up = glu(hp).reshape(b, s, d)
            us = glu(hs)
            cp = _conv_prompt(up, conv_w_dw, conv_b_dw, conv_ln_g, conv_ln_b, j, cfg=cfg)
            cs = _conv_decode(state_conv, us, conv_w_dw, conv_b_dw, conv_ln_g, conv_ln_b, j)
            proj = functools.partial(_mm, w=conv_w_out, li=j, col_offsets=(0,), n_cols=d, tm=cfg.tm,
                                     tn=cfg.tn, out_dtype=F32, bias=conv_b_out, res_scale=1.0)
            xp = proj(cp.reshape(b * s, d), res=xp)
            xs = proj(cs, res=xs)
            k = CONV_WIDTH - 1
            cp_l.append(up[:, s - k:])
            cs_l.append(jnp.concatenate([state_conv[:, j, 1:], us[:, None, :]], axis=1))
        xp = _cross_attn(xp, ln_cross, wq_bf, mem_k, mem_v, wo_bf, i, cfg=cfg, rows_per_mem=s)
        xs = _cross_attn(xs, ln_cross, wq_bf, mem_k_dec, mem_v_dec, wo_bf, i, cfg=cfg, rows_per_mem=0)
        xp = ffn(xp, ln_ffn2, ffn2_w_in, ffn2_w_out, i)
        xs = ffn(xs, ln_ffn2, ffn2_w_in, ffn2_w_out, i)

    g_final = ln_final.reshape(1, 1, d)
    y_prompt = _rmsnorm(xp, g_final, 0, F32, cfg.t_norm).reshape(b, s, d)
    y_sample = _rmsnorm(xs, g_final, 0, F32, cfg.t_norm).reshape(bd, 1, d)
    mh = (cfg.mem_heads, cfg.mem_head_dim)
    return (y_prompt, y_sample,
            jnp.stack(kp_l, axis=1), jnp.stack(vp_l, axis=1),
            jnp.stack(ks_l, axis=1), jnp.stack(vs_l, axis=1),
            jnp.stack(cp_l, axis=1), jnp.stack(cs_l, axis=1),
            mem_k.reshape(mem_k.shape[:3] + mh), mem_v.reshape(mem_v.shape[:3] + mh))


_CFG = Cfg(d_model=4096, head_dim=128, rot_dim=32, mem_heads=4, mem_head_dim=128, page_size=128,
           past_len=8192, tm=1024, tn=512, tm_ffn_out=512, tn_ffn_out=256, tq=512, t_conv=64,
           t_xattn=256, t_norm=256)


def kernel(x_prompt, x_sample, cache_k, cache_v, state_conv, cache_mem_k, cache_mem_v, page_table, mem_prompt, ln_ffn1, ffn1_w_in, ffn1_w_out, ln_mix, attn_w_qkv, attn_w_o, attn_lambda, attn_subln, conv_w_in, conv_b_in, conv_w_dw, conv_b_dw, conv_ln_g, conv_ln_b, conv_w_out, conv_b_out, ln_cross, ln_mem, cross_w_q, cross_w_kv, cross_w_o, ln_ffn2, ffn2_w_in, ffn2_w_out, ln_final):
    return _forward(_CFG, x_prompt, x_sample, cache_k, cache_v, state_conv, cache_mem_k, cache_mem_v,
                    page_table, mem_prompt, ln_ffn1, ffn1_w_in, ffn1_w_out, ln_mix, attn_w_qkv,
                    attn_w_o, attn_lambda, attn_subln, conv_w_in, conv_b_in, conv_w_dw, conv_b_dw,
                    conv_ln_g, conv_ln_b, conv_w_out, conv_b_out, ln_cross, ln_mem, cross_w_q,
                    cross_w_kv, cross_w_o, ln_ffn2, ffn2_w_in, ffn2_w_out, ln_final)
```

```python
import functools
import math
from typing import NamedTuple

import jax
import jax.numpy as jnp
from jax import lax
from jax.experimental import pallas as pl
from jax.experimental.pallas import tpu as pltpu

F32 = jnp.float32
BF16 = jnp.bfloat16

V7X_LANES = 128
V7X_SUBLANES = 8
V7X_VMEM_LIMIT_CAP = 56 * 1024 * 1024

N_MIXERS = 2
CONV_WIDTH = 31
ROPE_THETA = 500000.0
RMS_EPS = 1e-6
SUBLN_EPS = 1e-5
LN_EPS = 1e-5


class Cfg(NamedTuple):
    d_model: int
    head_dim: int
    rot_dim: int
    mem_heads: int
    mem_head_dim: int
    past_len: int
    tm: int
    tn: int
    tm_ffn_in: int
    tm_ffn_out: int
    tn_ffn_out: int
    tq: int
    t_conv: int
    t_xattn: int
    t_norm: int


def _nbytes(shape, dtype):
    return math.prod(shape) * jnp.dtype(dtype).itemsize


def _params(semantics, vmem_bytes):
    limit = min(V7X_VMEM_LIMIT_CAP, max(32 * 1024 * 1024, int(vmem_bytes)))
    return pltpu.CompilerParams(dimension_semantics=semantics, vmem_limit_bytes=limit)


def _rms(x, eps):
    return x * lax.rsqrt(jnp.mean(x * x, axis=-1, keepdims=True) + eps)


def _rmsnorm_body(x_ref, xs_ref, g_ref, o_ref, os_ref):
    o_ref[...] = (_rms(x_ref[...], RMS_EPS) * g_ref[...]).astype(o_ref.dtype)
    os_ref[...] = (_rms(xs_ref[...], RMS_EPS) * g_ref[...]).astype(os_ref.dtype)


def _rmsnorm(x, xs, g3, li, out_dtype, t_norm):
    m, d = x.shape
    ms = xs.shape[0]
    tr = min(m, t_norm)
    return pl.pallas_call(
        _rmsnorm_body,
        grid=(m // tr,),
        in_specs=[pl.BlockSpec((tr, d), lambda i: (i, 0)),
                  pl.BlockSpec((ms, d), lambda i: (0, 0)),
                  pl.BlockSpec((None, 1, d), lambda i: (li, 0, 0))],
        out_specs=[pl.BlockSpec((tr, d), lambda i: (i, 0)),
                   pl.BlockSpec((ms, d), lambda i: (0, 0))],
        out_shape=[jax.ShapeDtypeStruct((m, d), out_dtype),
                   jax.ShapeDtypeStruct((ms, d), out_dtype)],
        compiler_params=_params(("arbitrary",), 6 * _nbytes((tr, d), F32)),
        name="rmsnorm",
    )(x, xs, g3)


def _rope_lanes(y, c, s_lo, s_hi, rot_half):
    groups = []
    for g in range(y.shape[1] // V7X_LANES):
        yg = y[:, g * V7X_LANES:(g + 1) * V7X_LANES]
        groups.append(yg * c
                      + pltpu.roll(yg, V7X_LANES - rot_half, 1) * s_lo
                      + pltpu.roll(yg, rot_half, 1) * s_hi)
    return groups[0] if len(groups) == 1 else jnp.concatenate(groups, axis=1)


def _mm_body(*refs, n_w, has_bias, rot_half, res_scale, act):
    it = iter(refs)
    x_ref, xs_ref = next(it), next(it)
    w_refs = [next(it) for _ in range(n_w)]
    b_refs = [next(it) for _ in range(n_w)] if has_bias else []
    rope_refs = [next(it) for _ in range(3)] if rot_half else []
    rope_s_refs = [next(it) for _ in range(3)] if rot_half else []
    res_ref, res_s_ref = (next(it), next(it)) if res_scale is not None else (None, None)
    o_ref, os_ref = next(it), next(it)

    wb = [w_ref[...].astype(BF16) for w_ref in w_refs]

    def rows(x, rope_r, res_r):
        ys = []
        for t in range(n_w):
            y = jnp.dot(x, wb[t], preferred_element_type=F32)
            if has_bias:
                y = y + b_refs[t][...]
            ys.append(y)
        if act == "swiglu":
            y = jax.nn.silu(ys[0]) * ys[1]
        elif act == "glu":
            y = ys[0] * jax.nn.sigmoid(ys[1])
        else:
            y = ys[0]
        if rot_half:
            y = _rope_lanes(y, rope_r[0][...], rope_r[1][...], rope_r[2][...], rot_half)
        if res_scale is not None:
            y = res_r[...] + res_scale * y
        return y

    o_ref[...] = rows(x_ref[...], rope_refs, res_ref).astype(o_ref.dtype)

    @pl.when(pl.program_id(0) == 0)
    def _decode_rows():
        os_ref[pl.program_id(1)] = rows(xs_ref[...], rope_s_refs, res_s_ref).astype(os_ref.dtype)


def _mm(x, xs, w, li, *, col_offsets, n_cols, tm, tn, out_dtype, out_dtype_s, bias=None,
        rope=None, rope_s=None, rot_half=0, res=None, res_s=None, res_scale=None, act=None,
        single_buffer_x=False):
    m, k = x.shape
    ms = xs.shape[0]
    tm = min(tm, m)
    tn = min(tn, n_cols)
    n_w = len(col_offsets)
    nj = n_cols // tn
    x_mode = dict(pipeline_mode=pl.Buffered(1)) if single_buffer_x else {}
    in_specs = [pl.BlockSpec((tm, k), lambda i, j: (i, 0), **x_mode),
                pl.BlockSpec((ms, k), lambda i, j: (0, 0))]
    args = [x, xs]
    for off in col_offsets:
        in_specs.append(pl.BlockSpec((None, k, tn), lambda i, j, o=off // tn: (li, 0, j + o)))
        args.append(w)
    if bias is not None:
        for off in col_offsets:
            in_specs.append(pl.BlockSpec((None, 1, tn), lambda i, j, o=off // tn: (li, 0, j + o)))
            args.append(bias)
    if rope is not None:
        nblk = rope[0].shape[0] // tm
        for tab in rope:
            in_specs.append(pl.BlockSpec((tm, V7X_LANES), lambda i, j: (i % nblk, 0)))
            args.append(tab)
        for tab in rope_s:
            in_specs.append(pl.BlockSpec((ms, V7X_LANES), lambda i, j: (0, 0)))
            args.append(tab)
    if res is not None:
        in_specs.append(pl.BlockSpec((tm, tn), lambda i, j: (i, j)))
        in_specs.append(pl.BlockSpec((ms, tn), lambda i, j: (0, j)))
        args += [res, res_s]
    w_cast = _nbytes((k, tn), BF16) if w.dtype != BF16 else 0
    vmem = ((1 if single_buffer_x else 2) * _nbytes((tm, k), BF16)
            + n_w * (2 * _nbytes((k, tn), w.dtype) + w_cast)
            + (4 + 2 * n_w) * _nbytes((tm, tn), F32) + (6 << 20))
    body = functools.partial(_mm_body, n_w=n_w, has_bias=bias is not None,
                             rot_half=rot_half if rope is not None else 0,
                             res_scale=res_scale if res is not None else None, act=act)
    out, out_s = pl.pallas_call(
        body,
        grid=(m // tm, nj),
        in_specs=in_specs,
        out_specs=[pl.BlockSpec((tm, tn), lambda i, j: (i, j)),
                   pl.BlockSpec((nj, ms, tn), lambda i, j: (0, 0, 0))],
        out_shape=[jax.ShapeDtypeStruct((m, n_cols), out_dtype),
                   jax.ShapeDtypeStruct((nj, ms, tn), out_dtype_s)],
        compiler_params=_params(("arbitrary", "arbitrary"), vmem),
        name="matmul",
    )(*args)
    return out, out_s.transpose(1, 0, 2).reshape(ms, n_cols)


def _diff_lambda(lam, lam_init):
    a = jnp.sum(lam[0:1] * lam[1:2], axis=-1, keepdims=True)
    b = jnp.sum(lam[2:3] * lam[3:4], axis=-1, keepdims=True)
    return jnp.exp(a) - jnp.exp(b) + lam_init


def _flash_body(qi_ref, ki_ref, q_ref, k_ref, v_ref, lam_ref, g_ref, o_ref, m_sc, l_sc, acc_sc, *,
                hd, lam_init):
    qi = qi_ref[pl.program_id(2)]
    ki = ki_ref[pl.program_id(2)]
    scale = hd ** -0.5

    @pl.when(ki == 0)
    def _init():
        m_sc[...] = jnp.full(m_sc.shape, -jnp.inf, F32)
        l_sc[...] = jnp.zeros(l_sc.shape, F32)
        acc_sc[...] = jnp.zeros(acc_sc.shape, F32)

    def step(masked):
        q = q_ref[0]
        k = k_ref[0].astype(BF16)
        v = v_ref[0].astype(BF16)
        for c in range(2):
            s = lax.dot_general(q[:, c * hd:(c + 1) * hd], k[:, c * hd:(c + 1) * hd],
                                (((1,), (1,)), ((), ())), preferred_element_type=F32) * scale
            if masked:
                row = lax.broadcasted_iota(jnp.int32, s.shape, 0)
                col = lax.broadcasted_iota(jnp.int32, s.shape, 1)
                s = jnp.where(col <= row, s, -jnp.inf)
            m_prev = m_sc[c]
            m_new = jnp.maximum(m_prev, jnp.max(s, axis=-1, keepdims=True))
            alpha = jnp.exp(m_prev - m_new)
            p = jnp.exp(s - m_new)
            l_sc[c] = alpha * l_sc[c] + jnp.sum(p, axis=-1, keepdims=True)
            acc_sc[c] = alpha * acc_sc[c] + jnp.dot(p.astype(BF16), v, preferred_element_type=F32)
            m_sc[c] = m_new

    @pl.when(ki < qi)
    def _off_diagonal():
        step(False)

    @pl.when(ki == qi)
    def _diagonal():
        step(True)
        lam = _diff_lambda(lam_ref[...], lam_init)
        o = acc_sc[0] / l_sc[0] - lam * (acc_sc[1] / l_sc[1])
        o = _rms(o, SUBLN_EPS) * g_ref[...] * (1.0 - lam_init)
        o_ref[0] = o.astype(o_ref.dtype)


def _flash_diff_attn(q, k, v, lam, g, li, *, cfg, lam_init):
    b, s, d = q.shape
    hd = cfg.head_dim
    hw = 2 * hd
    t = min(cfg.tq, s)
    nq = s // t
    pairs = [(qi, ki) for qi in range(nq) for ki in range(qi + 1)]
    qi_tab = jnp.asarray([p[0] for p in pairs], jnp.int32)
    ki_tab = jnp.asarray([p[1] for p in pairs], jnp.int32)
    body = functools.partial(_flash_body, hd=hd, lam_init=lam_init)
    grid_spec = pltpu.PrefetchScalarGridSpec(
        num_scalar_prefetch=2,
        grid=(b, d // hw, len(pairs)),
        in_specs=[pl.BlockSpec((1, t, hw), lambda bi, h, p, qt, kt: (bi, qt[p], h)),
                  pl.BlockSpec((1, t, hw), lambda bi, h, p, qt, kt: (bi, kt[p], h)),
                  pl.BlockSpec((1, t, hw), lambda bi, h, p, qt, kt: (bi, kt[p], h)),
                  pl.BlockSpec((None, 4, hd), lambda bi, h, p, qt, kt: (li, 0, 0)),
                  pl.BlockSpec((None, 1, hw), lambda bi, h, p, qt, kt: (li, 0, 0))],
        out_specs=pl.BlockSpec((1, t, hw), lambda bi, h, p, qt, kt: (bi, qt[p], h)),
        scratch_shapes=[pltpu.VMEM((2, t, 1), F32), pltpu.VMEM((2, t, 1), F32),
                        pltpu.VMEM((2, t, hw), F32)],
    )
    return pl.pallas_call(
        body,
        grid_spec=grid_spec,
        out_shape=jax.ShapeDtypeStruct((b, s, d), BF16),
        compiler_params=_params(("parallel", "parallel", "arbitrary"), 32 << 20),
        name="flash_diff_attn",
    )(qi_tab, ki_tab, q, k, v, lam, g)


def _decode_attn_body(pt_ref, q_ref, kn_ref, vn_ref, ck_ref, cv_ref, lam_ref, g_ref, o_ref,
                      m_sc, l_sc, acc_sc, *, lam_init, n_pages):
    del pt_ref
    p_idx = pl.program_id(1)
    page, nh, hw = cv_ref.shape[2:]
    hd = hw // 2
    scale = hd ** -0.5

    def component(ref, m):
        return ref[0, pl.ds(m, nh, stride=2), :]

    @pl.when(p_idx == 0)
    def _init():
        s0 = []
        for m in range(2):
            qb = component(q_ref, m).astype(BF16).astype(F32)
            kb = component(kn_ref, m).astype(BF16).astype(F32)
            s0.append(jnp.sum(qb * kb, axis=-1, keepdims=True) * scale)
        m_sc[...] = jnp.concatenate(s0, axis=0)
        l_sc[...] = jnp.ones(l_sc.shape, F32)
        vb = vn_ref[0].astype(BF16).astype(F32)
        acc_sc[...] = jnp.concatenate([vb, vb], axis=0)

    s = []
    for m in range(2):
        km = ck_ref[0, 0, :, pl.ds(m, nh, stride=2), :].reshape(page * nh, hd).astype(BF16)
        s.append(lax.dot_general(component(q_ref, m).astype(BF16), km, (((1,), (1,)), ((), ())),
                                 preferred_element_type=F32) * scale)
    s = jnp.concatenate(s, axis=0)
    row = lax.broadcasted_iota(jnp.int32, s.shape, 0)
    col = lax.broadcasted_iota(jnp.int32, s.shape, 1)
    s = jnp.where((col & (nh - 1)) == (row & (nh - 1)), s, -jnp.inf)
    m_prev = m_sc[...]
    m_new = jnp.maximum(m_prev, jnp.max(s, axis=-1, keepdims=True))
    alpha = jnp.exp(m_prev - m_new)
    p = jnp.exp(s - m_new)
    l_sc[...] = alpha * l_sc[...] + jnp.sum(p, axis=-1, keepdims=True)
    v2 = cv_ref[0, 0].reshape(page * nh, hw).astype(BF16)
    acc_sc[...] = alpha * acc_sc[...] + jnp.dot(p.astype(BF16), v2, preferred_element_type=F32)
    m_sc[...] = m_new

    @pl.when(p_idx == n_pages - 1)
    def _finish():
        lam = _diff_lambda(lam_ref[...], lam_init)
        a = acc_sc[...] / l_sc[...]
        o = a[0:nh] - lam * a[nh:2 * nh]
        o_ref[0] = _rms(o, SUBLN_EPS) * g_ref[...] * (1.0 - lam_init)


def _decode_diff_attn(q, k_new, v_new, cache_k, cache_v, page_table, lam, g, li, *, lam_init):
    bd, nc, hd = q.shape
    nh, hw = v_new.shape[1:]
    assert nh & (nh - 1) == 0, "head mask uses a power-of-two head count"
    page = cache_k.shape[2]
    n_pages = page_table.shape[1]
    body = functools.partial(_decode_attn_body, lam_init=lam_init, n_pages=n_pages)
    grid_spec = pltpu.PrefetchScalarGridSpec(
        num_scalar_prefetch=1,
        grid=(bd, n_pages),
        in_specs=[pl.BlockSpec((1, nc, hd), lambda b, p, pt: (b, 0, 0)),
                  pl.BlockSpec((1, nc, hd), lambda b, p, pt: (b, 0, 0)),
                  pl.BlockSpec((1, nh, hw), lambda b, p, pt: (b, 0, 0)),
                  pl.BlockSpec((1, 1, page, nc, hd),
                               lambda b, p, pt: (pt[b * n_pages + p], li, 0, 0, 0)),
                  pl.BlockSpec((1, 1, page, nh, hw),
                               lambda b, p, pt: (pt[b * n_pages + p], li, 0, 0, 0)),
                  pl.BlockSpec((None, 4, hd), lambda b, p, pt: (li, 0, 0)),
                  pl.BlockSpec((None, 1, hw), lambda b, p, pt: (li, 0, 0))],
        out_specs=pl.BlockSpec((1, nh, hw), lambda b, p, pt: (b, 0, 0)),
        scratch_shapes=[pltpu.VMEM((nc, 1), F32), pltpu.VMEM((nc, 1), F32),
                        pltpu.VMEM((nc, hw), F32)],
    )
    return pl.pallas_call(
        body,
        grid_spec=grid_spec,
        out_shape=jax.ShapeDtypeStruct((bd, nh, hw), F32),
        compiler_params=_params(("parallel", "arbitrary"),
                                8 * _nbytes((page, nc, hd), F32) + (8 << 20)),
        name="decode_diff_attn",
    )(page_table.reshape(-1), q, k_new, v_new, cache_k, cache_v, lam, g)


def _ln_silu(c, g, b):
    mu = jnp.mean(c, axis=-1, keepdims=True)
    var = jnp.mean(jnp.square(c - mu), axis=-1, keepdims=True)
    y = (c - mu) * lax.rsqrt(var + LN_EPS) * g + b
    return jax.nn.silu(y)


def _conv_body(cur_ref, prev_ref, wdw_ref, bdw_ref, g_ref, b_ref, o_ref, ext_sc, c_sc, *,
               ts, halo, chunk):
    si = pl.program_id(1)
    ext_sc[0:halo, :] = jnp.where(si > 0, prev_ref[0], 0.0)
    ext_sc[halo:halo + ts, :] = cur_ref[0]
    ext_sc[halo + ts:, :] = jnp.zeros((V7X_SUBLANES, ext_sc.shape[1]), F32)
    lead = halo - (CONV_WIDTH - 1)
    d = cur_ref.shape[2]
    for c0 in range(0, d, chunk):
        cols = slice(c0, c0 + chunk)
        acc = None
        for r in range(V7X_SUBLANES):
            part = None
            for w in range(CONV_WIDTH):
                if (lead + w) % V7X_SUBLANES != r:
                    continue
                a8 = lead + w - r
                term = ext_sc[a8:a8 + ts + V7X_SUBLANES, cols] * wdw_ref[w:w + 1, cols]
                part = term if part is None else part + term
            if part is not None:
                shifted = part[r:r + ts]
                acc = shifted if acc is None else acc + shifted
        c_sc[:, cols] = acc + bdw_ref[:, cols]
    o_ref[0] = _ln_silu(c_sc[...], g_ref[...], b_ref[...]).astype(o_ref.dtype)


def _conv_prompt(u, w_dw, b_dw, ln_g, ln_b, li, *, cfg):
    b, s, d = u.shape
    ts = min(cfg.t_conv, s)
    halo = 32
    per = ts // halo
    body = functools.partial(_conv_body, ts=ts, halo=halo, chunk=min(d, 256))
    vec = pl.BlockSpec((None, 1, d), lambda bi, si: (li, 0, 0))
    return pl.pallas_call(
        body,
        grid=(b, s // ts),
        in_specs=[pl.BlockSpec((1, ts, d), lambda bi, si: (bi, si, 0)),
                  pl.BlockSpec((1, halo, d), lambda bi, si: (bi, jnp.maximum(si * per - 1, 0), 0)),
                  pl.BlockSpec((None, CONV_WIDTH, d), lambda bi, si: (li, 0, 0)),
                  vec, vec, vec],
        out_specs=pl.BlockSpec((1, ts, d), lambda bi, si: (bi, si, 0)),
        out_shape=jax.ShapeDtypeStruct((b, s, d), BF16),
        scratch_shapes=[pltpu.VMEM((halo + ts + V7X_SUBLANES, d), F32), pltpu.VMEM((ts, d), F32)],
        compiler_params=_params(("parallel", "arbitrary"), 32 << 20),
        name="conv_prompt",
    )(u, u, w_dw, b_dw, ln_g, ln_b)


def _conv_decode_body(hist_ref, u_ref, wdw_ref, bdw_ref, g_ref, b_ref, o_ref):
    nb = hist_ref.shape[0]
    k = CONV_WIDTH - 1
    w_hist = wdw_ref[0:k, :]
    rows = [jnp.sum(hist_ref[bi] * w_hist, axis=0, keepdims=True) for bi in range(nb)]
    c = jnp.concatenate(rows, axis=0) + u_ref[...] * wdw_ref[k:k + 1, :] + bdw_ref[...]
    o_ref[...] = _ln_silu(c, g_ref[...], b_ref[...]).astype(o_ref.dtype)


def _conv_decode(state_conv, u, w_dw, b_dw, ln_g, ln_b, li):
    bd, _, k, d = state_conv.shape
    vec = pl.BlockSpec((None, 1, d), lambda i: (li, 0, 0))
    return pl.pallas_call(
        _conv_decode_body,
        grid=(1,),
        in_specs=[pl.BlockSpec((bd, None, k, d), lambda i: (0, li, 0, 0)),
                  pl.BlockSpec((bd, d), lambda i: (0, 0)),
                  pl.BlockSpec((None, CONV_WIDTH, d), lambda i: (li, 0, 0)),
                  vec, vec, vec],
        out_specs=pl.BlockSpec((bd, d), lambda i: (0, 0)),
        out_shape=jax.ShapeDtypeStruct((bd, d), BF16),
        compiler_params=_params(("arbitrary",), 32 << 20),
        name="conv_decode",
    )(state_conv, u, w_dw, b_dw, ln_g, ln_b)


def _xattn_body(x_ref, g_ref, wq_ref, mk_ref, mv_ref, wo_ref, gn_ref, o_ref, hn_ref, *,
                n_heads, hd, pick_row):
    x = x_ref[...]
    h = (_rms(x, RMS_EPS) * g_ref[...]).astype(BF16)
    q = jnp.dot(h, wq_ref[...], preferred_element_type=F32).astype(BF16)
    mk = mk_ref[0, 0].astype(BF16)
    mv = mv_ref[0, 0].astype(BF16)
    scale = hd ** -0.5
    heads = []
    for hh in range(n_heads):
        sl = slice(hh * hd, (hh + 1) * hd)
        s = lax.dot_general(q[:, sl], mk[:, sl], (((1,), (1,)), ((), ())),
                            preferred_element_type=F32) * scale
        e = jnp.exp(s - jnp.max(s, axis=-1, keepdims=True))
        p = e / jnp.sum(e, axis=-1, keepdims=True)
        heads.append(jnp.dot(p.astype(BF16), mv[:, sl], preferred_element_type=F32))
    o = jnp.concatenate(heads, axis=1).astype(BF16)
    y = x + jnp.dot(o, wo_ref[...], preferred_element_type=F32)
    if pick_row:
        row = lax.broadcasted_iota(jnp.int32, y.shape, 0)
        y = jnp.sum(jnp.where(row == pl.program_id(0), y, 0.0), axis=0, keepdims=True)
    o_ref[...] = y
    hn_ref[...] = (_rms(y, RMS_EPS) * gn_ref[...]).astype(hn_ref.dtype)


def _cross_attn(x, g3, wq, mk, mv, wo, gn3, li, *, cfg, rows_per_mem):
    m, d = x.shape
    n_heads, hd = cfg.mem_heads, cfg.mem_head_dim
    e = n_heads * hd
    t_mem = mk.shape[2]
    pick_row = rows_per_mem == 0
    if pick_row:
        tm, steps = m, m
        x_spec = pl.BlockSpec((m, d), lambda i: (0, 0))
        mem_idx = lambda i: (i, li, 0, 0)
        out_spec = pl.BlockSpec((None, 1, d), lambda i: (i, 0, 0))
        out_dims = (m, 1, d)
    else:
        tm = min(cfg.t_xattn, rows_per_mem)
        steps = m // tm
        per = rows_per_mem // tm
        x_spec = pl.BlockSpec((tm, d), lambda i: (i, 0))
        mem_idx = lambda i: (i // per, li, 0, 0)
        out_spec = pl.BlockSpec((tm, d), lambda i: (i, 0))
        out_dims = (m, d)
    vec = lambda i: (li, 0, 0)
    body = functools.partial(_xattn_body, n_heads=n_heads, hd=hd, pick_row=pick_row)
    y, hn = pl.pallas_call(
        body,
        grid=(steps,),
        in_specs=[x_spec,
                  pl.BlockSpec((None, 1, d), vec),
                  pl.BlockSpec((None, d, e), vec),
                  pl.BlockSpec((1, 1, t_mem, e), mem_idx),
                  pl.BlockSpec((1, 1, t_mem, e), mem_idx),
                  pl.BlockSpec((None, e, d), vec),
                  pl.BlockSpec((None, 1, d), vec)],
        out_specs=[out_spec, out_spec],
        out_shape=[jax.ShapeDtypeStruct(out_dims, F32), jax.ShapeDtypeStruct(out_dims, BF16)],
        compiler_params=_params(("parallel",), 10 * _nbytes((tm, d), F32)
                                + 4 * _nbytes((d, e), BF16) + (8 << 20)),
        name="cross_attn",
    )(x, g3, wq, mk, mv, wo, gn3)
    return y.reshape(m, d), hn.reshape(m, d)


def _memkv_body(mem_ref, g_ref, wk_ref, wv_ref, ok_ref, ov_ref):
    h = (_rms(mem_ref[0], RMS_EPS) * g_ref[...]).astype(BF16)
    ok_ref[0, 0] = jnp.dot(h, wk_ref[...].astype(BF16), preferred_element_type=F32)
    ov_ref[0, 0] = jnp.dot(h, wv_ref[...].astype(BF16), preferred_element_type=F32)


def _mem_kv(mem, g3, w_kv, *, cfg):
    b, t, d = mem.shape
    n_layers = w_kv.shape[0]
    e = w_kv.shape[2] // 2
    tn = min(e, 256)
    nj = e // tn
    out = jax.ShapeDtypeStruct((b, n_layers, t, e), F32)
    out_spec = pl.BlockSpec((1, 1, t, tn), lambda l, j, bi: (bi, l, 0, j))
    return pl.pallas_call(
        _memkv_body,
        grid=(n_layers, nj, b),
        in_specs=[pl.BlockSpec((1, t, d), lambda l, j, bi: (bi, 0, 0)),
                  pl.BlockSpec((None, 1, d), lambda l, j, bi: (l, 0, 0)),
                  pl.BlockSpec((None, d, tn), lambda l, j, bi: (l, 0, j)),
                  pl.BlockSpec((None, d, tn), lambda l, j, bi: (l, 0, nj + j))],
        out_specs=[out_spec, out_spec],
        out_shape=[out, out],
        compiler_params=_params(("parallel", "parallel", "parallel"),
                                6 * _nbytes((d, tn), F32) + 4 * _nbytes((t, d), F32) + (8 << 20)),
        name="mem_kv",
    )(mem, g3, w_kv, w_kv)


def _rope_tables(pos, cfg):
    half = cfg.rot_dim // 2
    inv_freq = jnp.power(jnp.float32(ROPE_THETA),
                         -(jnp.arange(0, cfg.rot_dim, 2, dtype=F32) / cfg.rot_dim))
    ang = pos.astype(F32)[:, None] * inv_freq[None, :]
    cos, sin = jnp.cos(ang), jnp.sin(ang)
    pad = V7X_LANES - cfg.rot_dim
    n = pos.shape[0]
    c = jnp.concatenate([cos, cos, jnp.ones((n, pad), F32)], axis=1)
    s_lo = jnp.concatenate([-sin, jnp.zeros((n, half + pad), F32)], axis=1)
    s_hi = jnp.concatenate([jnp.zeros((n, half), F32), sin, jnp.zeros((n, pad), F32)], axis=1)
    return c, s_lo, s_hi


def _forward(cfg, x_prompt, x_sample, cache_k, cache_v, state_conv, cache_mem_k, cache_mem_v,
             page_table, mem_prompt, ln_ffn1, ffn1_w_in, ffn1_w_out, ln_mix, attn_w_qkv, attn_w_o,
             attn_lambda, attn_subln, conv_w_in, conv_b_in, conv_w_dw, conv_b_dw, conv_ln_g,
             conv_ln_b, conv_w_out, conv_b_out, ln_cross, ln_mem, cross_w_q, cross_w_kv, cross_w_o,
             ln_ffn2, ffn2_w_in, ffn2_w_out, ln_final):
    b, s, d = x_prompt.shape
    bd = x_sample.shape[0]
    depth = ln_ffn1.shape[0]
    d_ff = ffn1_w_out.shape[1]
    hd = cfg.head_dim
    hw = 2 * hd
    nc = d // hd
    e = cfg.mem_heads * cfg.mem_head_dim
    rot_half = cfg.rot_dim // 2

    row3 = lambda a: a.reshape(a.shape[0], 1, a.shape[-1])
    ln_ffn1, ln_mix, ln_cross, ln_mem, ln_ffn2 = map(row3, (ln_ffn1, ln_mix, ln_cross, ln_mem, ln_ffn2))
    conv_b_in, conv_b_dw, conv_ln_g, conv_ln_b, conv_b_out = map(
        row3, (conv_b_in, conv_b_dw, conv_ln_g, conv_ln_b, conv_b_out))
    subln = row3(attn_subln)
    wq_bf = cross_w_q.astype(BF16)
    wo_bf = cross_w_o.astype(BF16)
    ffn1_w_out = ffn1_w_out.astype(BF16)
    ffn2_w_out = ffn2_w_out.astype(BF16)
    mem_k_dec = cache_mem_k.reshape(bd, depth, -1, e)
    mem_v_dec = cache_mem_v.reshape(bd, depth, -1, e)

    rope_p = _rope_tables(jnp.arange(s, dtype=jnp.int32), cfg)
    rope_s = _rope_tables(jnp.full((bd,), cfg.past_len, jnp.int32), cfg)

    mem_k, mem_v = _mem_kv(mem_prompt, ln_mem, cross_w_kv, cfg=cfg)

    def ffn(xp, xs, hp, hs, w_in, w_out, li):
        ap, as_ = _mm(hp, hs, w_in, li, col_offsets=(0, d_ff), n_cols=d_ff, tm=cfg.tm_ffn_in,
                      tn=cfg.tn // 2, out_dtype=BF16, out_dtype_s=BF16, act="swiglu",
                      single_buffer_x=True)
        return _mm(ap, as_, w_out, li, col_offsets=(0,), n_cols=d, tm=cfg.tm_ffn_out,
                   tn=cfg.tn_ffn_out, out_dtype=F32, out_dtype_s=F32, res=xp, res_s=xs,
                   res_scale=0.5, single_buffer_x=True)

    def proj(hp, hs, w, li, xp, xs, bias=None):
        return _mm(hp, hs, w, li, col_offsets=(0,), n_cols=d, tm=cfg.tm, tn=cfg.tn,
                   out_dtype=F32, out_dtype_s=F32, bias=bias, res=xp, res_s=xs, res_scale=1.0)

    xp = x_prompt.reshape(b * s, d)
    xs = x_sample.reshape(bd, d)
    kp_l, vp_l, ks_l, vs_l, cp_l, cs_l = [], [], [], [], [], []
    for i in range(depth):
        j = i // N_MIXERS
        hp, hs = _rmsnorm(xp, xs, ln_ffn1, i, BF16, cfg.t_norm)
        xp, xs = ffn(xp, xs, hp, hs, ffn1_w_in, ffn1_w_out, i)
        hp, hs = _rmsnorm(xp, xs, ln_mix, i, BF16, cfg.t_norm)
        if i % N_MIXERS == 0:
            lam_init = 0.8 - 0.6 * math.exp(-0.3 * i)
            mm = functools.partial(_mm, hp, hs, attn_w_qkv, j, n_cols=d, tm=cfg.tm, tn=cfg.tn,
                                   out_dtype_s=F32)
            rope = dict(rope=rope_p, rope_s=rope_s, rot_half=rot_half)
            qp, qs = mm(col_offsets=(0,), out_dtype=BF16, **rope)
            kp, ks = mm(col_offsets=(d,), out_dtype=F32, **rope)
            vp, vs = mm(col_offsets=(2 * d,), out_dtype=F32)
            op = _flash_diff_attn(qp.reshape(b, s, d), kp.reshape(b, s, d),
                                  vp.reshape(b, s, d), attn_lambda, subln, j,
                                  cfg=cfg, lam_init=lam_init)
            os_ = _decode_diff_attn(qs.reshape(bd, nc, hd), ks.reshape(bd, nc, hd),
                                    vs.reshape(bd, nc // 2, hw), cache_k, cache_v, page_table,
                                    attn_lambda, subln, j, lam_init=lam_init)
            xp, xs = proj(op.reshape(b * s, d), os_.reshape(bd, d).astype(BF16), attn_w_o, j,
                          xp, xs)
            kp_l.append(kp.reshape(b, s, nc, hd))
            vp_l.append(vp.reshape(b, s, nc // 2, hw))
            ks_l.append(ks.reshape(bd, 1, nc, hd))
            vs_l.append(vs.reshape(bd, 1, nc // 2, hw))
        else:
            up, us = _mm(hp, hs, conv_w_in, j, col_offsets=(0, d), n_cols=d, tm=cfg.tm,
                         tn=cfg.tn // 2, out_dtype=F32, out_dtype_s=F32, bias=conv_b_in, act="glu")
            up = up.reshape(b, s, d)
            cp = _conv_prompt(up, conv_w_dw, conv_b_dw, conv_ln_g, conv_ln_b, j, cfg=cfg)
            cs = _conv_decode(state_conv, us, conv_w_dw, conv_b_dw, conv_ln_g, conv_ln_b, j)
            xp, xs = proj(cp.reshape(b * s, d), cs, conv_w_out, j, xp, xs, bias=conv_b_out)
            k = CONV_WIDTH - 1
            cp_l.append(up[:, s - k:])
            cs_l.append(jnp.concatenate([state_conv[:, j, 1:], us[:, None, :]], axis=1))
        xp, hp = _cross_attn(xp, ln_cross, wq_bf, mem_k, mem_v, wo_bf, ln_ffn2, i, cfg=cfg,
                             rows_per_mem=s)
        xs, hs = _cross_attn(xs, ln_cross, wq_bf, mem_k_dec, mem_v_dec, wo_bf, ln_ffn2, i, cfg=cfg,
                             rows_per_mem=0)
        xp, xs = ffn(xp, xs, hp, hs, ffn2_w_in, ffn2_w_out, i)

    y_prompt, y_sample = _rmsnorm(xp, xs, ln_final.reshape(1, 1, d), 0, F32, cfg.t_norm)
    y_prompt = y_prompt.reshape(b, s, d)
    y_sample = y_sample.reshape(bd, 1, d)
    mh = (cfg.mem_heads, cfg.mem_head_dim)
    return (y_prompt, y_sample,
            jnp.stack(kp_l, axis=1), jnp.stack(vp_l, axis=1),
            jnp.stack(ks_l, axis=1), jnp.stack(vs_l, axis=1),
            jnp.stack(cp_l, axis=1), jnp.stack(cs_l, axis=1),
            mem_k.reshape(mem_k.shape[:3] + mh), mem_v.reshape(mem_v.shape[:3] + mh))


_CFG = Cfg(d_model=4096, head_dim=128, rot_dim=32, mem_heads=4, mem_head_dim=128,
           past_len=8192, tm=1024, tn=512, tm_ffn_in=2048, tm_ffn_out=1024, tn_ffn_out=256,
           tq=512, t_conv=64, t_xattn=256, t_norm=256)


def kernel(x_prompt, x_sample, cache_k, cache_v, state_conv, cache_mem_k, cache_mem_v, page_table, mem_prompt, ln_ffn1, ffn1_w_in, ffn1_w_out, ln_mix, attn_w_qkv, attn_w_o, attn_lambda, attn_subln, conv_w_in, conv_b_in, conv_w_dw, conv_b_dw, conv_ln_g, conv_ln_b, conv_w_out, conv_b_out, ln_cross, ln_mem, cross_w_q, cross_w_kv, cross_w_o, ln_ffn2, ffn2_w_in, ffn2_w_out, ln_final):
    return _forward(_CFG, x_prompt, x_sample, cache_k, cache_v, state_conv, cache_mem_k, cache_mem_v,
                    page_table, mem_prompt, ln_ffn1, ffn1_w_in, ffn1_w_out, ln_mix, attn_w_qkv,
                    attn_w_o, attn_lambda, attn_subln, conv_w_in, conv_b_in, conv_w_dw, conv_b_dw,
                    conv_ln_g, conv_ln_b, conv_w_out, conv_b_out, ln_cross, ln_mem, cross_w_q,
                    cross_w_kv, cross_w_o, ln_ffn2, ffn2_w_in, ffn2_w_out, ln_final)
```

```python
import functools
import math
from typing import NamedTuple

import jax
import jax.numpy as jnp
from jax import lax
from jax.experimental import pallas as pl
from jax.experimental.pallas import tpu as pltpu

F32 = jnp.float32
BF16 = jnp.bfloat16

V7X_LANES = 128
V7X_SUBLANES = 8
V7X_VMEM_LIMIT_CAP = 56 * 1024 * 1024

N_MIXERS = 2
CONV_WIDTH = 31
ROPE_THETA = 500000.0
RMS_EPS = 1e-6
SUBLN_EPS = 1e-5
LN_EPS = 1e-5


class Cfg(NamedTuple):
    d_model: int
    head_dim: int
    rot_dim: int
    mem_heads: int
    mem_head_dim: int
    past_len: int
    tm: int
    tn: int
    tm_ffn_in: int
    tm_ffn_out: int
    tn_ffn_out: int
    tq: int
    t_conv: int
    t_xattn: int
    t_norm: int


def _nbytes(shape, dtype):
    return math.prod(shape) * jnp.dtype(dtype).itemsize


def _params(semantics, vmem_bytes):
    limit = min(V7X_VMEM_LIMIT_CAP, max(32 * 1024 * 1024, int(vmem_bytes)))
    return pltpu.CompilerParams(dimension_semantics=semantics, vmem_limit_bytes=limit)


def _rms(x, eps):
    return x * lax.rsqrt(jnp.mean(x * x, axis=-1, keepdims=True) + eps)


def _rmsnorm_body(x_ref, xs_ref, g_ref, o_ref, os_ref):
    o_ref[...] = (_rms(x_ref[...], RMS_EPS) * g_ref[...]).astype(o_ref.dtype)
    os_ref[...] = (_rms(xs_ref[...], RMS_EPS) * g_ref[...]).astype(os_ref.dtype)


def _rmsnorm(x, xs, g3, li, out_dtype, t_norm):
    m, d = x.shape
    ms = xs.shape[0]
    tr = min(m, t_norm)
    return pl.pallas_call(
        _rmsnorm_body,
        grid=(m // tr,),
        in_specs=[pl.BlockSpec((tr, d), lambda i: (i, 0)),
                  pl.BlockSpec((ms, d), lambda i: (0, 0)),
                  pl.BlockSpec((None, 1, d), lambda i: (li, 0, 0))],
        out_specs=[pl.BlockSpec((tr, d), lambda i: (i, 0)),
                   pl.BlockSpec((ms, d), lambda i: (0, 0))],
        out_shape=[jax.ShapeDtypeStruct((m, d), out_dtype),
                   jax.ShapeDtypeStruct((ms, d), out_dtype)],
        compiler_params=_params(("arbitrary",), 6 * _nbytes((tr, d), F32)),
        name="rmsnorm",
    )(x, xs, g3)


def _rope_lanes(y, c, s_lo, s_hi, rot_half):
    groups = []
    for g in range(y.shape[1] // V7X_LANES):
        yg = y[:, g * V7X_LANES:(g + 1) * V7X_LANES]
        groups.append(yg * c
                      + pltpu.roll(yg, V7X_LANES - rot_half, 1) * s_lo
                      + pltpu.roll(yg, rot_half, 1) * s_hi)
    return groups[0] if len(groups) == 1 else jnp.concatenate(groups, axis=1)


def _mm_body(*refs, n_w, has_bias, rot_half, res_scale, act, has_cast):
    it = iter(refs)
    x_ref, xs_ref = next(it), next(it)
    w_refs = [next(it) for _ in range(n_w)]
    b_refs = [next(it) for _ in range(n_w)] if has_bias else []
    rope_refs = [next(it) for _ in range(3)] if rot_half else []
    rope_s_refs = [next(it) for _ in range(3)] if rot_half else []
    res_ref, res_s_ref = (next(it), next(it)) if res_scale is not None else (None, None)
    cast_ref = next(it) if has_cast else None
    o_ref, os_ref = next(it), next(it)
    if has_cast:
        next(it)[...] = cast_ref[...].astype(BF16)

    wb = [w_ref[...].astype(BF16) for w_ref in w_refs]

    def rows(x, rope_r, res_r):
        ys = []
        for t in range(n_w):
            y = jnp.dot(x, wb[t], preferred_element_type=F32)
            if has_bias:
                y = y + b_refs[t][...]
            ys.append(y)
        if act == "swiglu":
            y = jax.nn.silu(ys[0]) * ys[1]
        elif act == "glu":
            y = ys[0] * jax.nn.sigmoid(ys[1])
        else:
            y = ys[0]
        if rot_half:
            y = _rope_lanes(y, rope_r[0][...], rope_r[1][...], rope_r[2][...], rot_half)
        if res_scale is not None:
            y = res_r[...] + res_scale * y
        return y

    o_ref[...] = rows(x_ref[...], rope_refs, res_ref).astype(o_ref.dtype)

    @pl.when(pl.program_id(0) == 0)
    def _decode_rows():
        os_ref[pl.program_id(1)] = rows(xs_ref[...], rope_s_refs, res_s_ref).astype(os_ref.dtype)


def _mm(x, xs, w, li, *, col_offsets, n_cols, tm, tn, out_dtype, out_dtype_s, bias=None,
        rope=None, rope_s=None, rot_half=0, res=None, res_s=None, res_scale=None, act=None,
        single_buffer_x=False, cast_next=None):
    m, k = x.shape
    ms = xs.shape[0]
    tm = min(tm, m)
    tn = min(tn, n_cols)
    n_w = len(col_offsets)
    nj = n_cols // tn
    x_mode = dict(pipeline_mode=pl.Buffered(1)) if single_buffer_x else {}
    in_specs = [pl.BlockSpec((tm, k), lambda i, j: (i, 0), **x_mode),
                pl.BlockSpec((ms, k), lambda i, j: (0, 0))]
    args = [x, xs]
    for off in col_offsets:
        in_specs.append(pl.BlockSpec((None, k, tn), lambda i, j, o=off // tn: (li, 0, j + o)))
        args.append(w)
    if bias is not None:
        for off in col_offsets:
            in_specs.append(pl.BlockSpec((None, 1, tn), lambda i, j, o=off // tn: (li, 0, j + o)))
            args.append(bias)
    if rope is not None:
        nblk = rope[0].shape[0] // tm
        for tab in rope:
            in_specs.append(pl.BlockSpec((tm, V7X_LANES), lambda i, j: (i % nblk, 0)))
            args.append(tab)
        for tab in rope_s:
            in_specs.append(pl.BlockSpec((ms, V7X_LANES), lambda i, j: (0, 0)))
            args.append(tab)
    if res is not None:
        in_specs.append(pl.BlockSpec((tm, tn), lambda i, j: (i, j)))
        in_specs.append(pl.BlockSpec((ms, tn), lambda i, j: (0, j)))
        args += [res, res_s]
    out_specs = [pl.BlockSpec((tm, tn), lambda i, j: (i, j)),
                 pl.BlockSpec((nj, ms, tn), lambda i, j: (0, 0, 0))]
    out_shape = [jax.ShapeDtypeStruct((m, n_cols), out_dtype),
                 jax.ShapeDtypeStruct((nj, ms, tn), out_dtype_s)]
    if cast_next is not None:
        r, c = cast_next.shape[1:]
        steps = (m // tm) * nj
        rc = r // steps
        assert rc * steps == r and rc % (2 * V7X_SUBLANES) == 0, (r, steps)
        in_specs.append(pl.BlockSpec((None, rc, c), lambda i, j: (li, i * nj + j, 0)))
        args.append(cast_next)
        out_specs.append(pl.BlockSpec((rc, c), lambda i, j: (i * nj + j, 0)))
        out_shape.append(jax.ShapeDtypeStruct((r, c), BF16))
    w_cast = _nbytes((k, tn), BF16) if w.dtype != BF16 else 0
    vmem = ((1 if single_buffer_x else 2) * _nbytes((tm, k), BF16)
            + n_w * (2 * _nbytes((k, tn), w.dtype) + w_cast)
            + (4 + 2 * n_w) * _nbytes((tm, tn), F32) + (6 << 20))
    body = functools.partial(_mm_body, n_w=n_w, has_bias=bias is not None,
                             rot_half=rot_half if rope is not None else 0,
                             res_scale=res_scale if res is not None else None, act=act,
                             has_cast=cast_next is not None)
    out, out_s, *cast = pl.pallas_call(
        body,
        grid=(m // tm, nj),
        in_specs=in_specs,
        out_specs=out_specs,
        out_shape=out_shape,
        compiler_params=_params(("arbitrary", "arbitrary"), vmem),
        name="matmul",
    )(*args)
    return (out, out_s.transpose(1, 0, 2).reshape(ms, n_cols), *cast)


def _diff_lambda(lam, lam_init):
    a = jnp.sum(lam[0:1] * lam[1:2], axis=-1, keepdims=True)
    b = jnp.sum(lam[2:3] * lam[3:4], axis=-1, keepdims=True)
    return jnp.exp(a) - jnp.exp(b) + lam_init


def _flash_body(qi_ref, ki_ref, q_ref, k_ref, v_ref, lam_ref, g_ref, o_ref, m_sc, l_sc, acc_sc, *,
                hd, lam_init):
    qi = qi_ref[pl.program_id(2)]
    ki = ki_ref[pl.program_id(2)]
    scale = hd ** -0.5

    @pl.when(ki == 0)
    def _init():
        m_sc[...] = jnp.full(m_sc.shape, -jnp.inf, F32)
        l_sc[...] = jnp.zeros(l_sc.shape, F32)
        acc_sc[...] = jnp.zeros(acc_sc.shape, F32)

    def step(masked):
        q = q_ref[0]
        k = k_ref[0].astype(BF16)
        v = v_ref[0].astype(BF16)
        for c in range(2):
            s = lax.dot_general(q[:, c * hd:(c + 1) * hd], k[:, c * hd:(c + 1) * hd],
                                (((1,), (1,)), ((), ())), preferred_element_type=F32) * scale
            if masked:
                row = lax.broadcasted_iota(jnp.int32, s.shape, 0)
                col = lax.broadcasted_iota(jnp.int32, s.shape, 1)
                s = jnp.where(col <= row, s, -jnp.inf)
            m_prev = m_sc[c]
            m_new = jnp.maximum(m_prev, jnp.max(s, axis=-1, keepdims=True))
            alpha = jnp.exp(m_prev - m_new)
            p = jnp.exp(s - m_new)
            l_sc[c] = alpha * l_sc[c] + jnp.sum(p, axis=-1, keepdims=True)
            acc_sc[c] = alpha * acc_sc[c] + jnp.dot(p.astype(BF16), v, preferred_element_type=F32)
            m_sc[c] = m_new

    @pl.when(ki < qi)
    def _off_diagonal():
        step(False)

    @pl.when(ki == qi)
    def _diagonal():
        step(True)
        lam = _diff_lambda(lam_ref[...], lam_init)
        o = acc_sc[0] / l_sc[0] - lam * (acc_sc[1] / l_sc[1])
        o = _rms(o, SUBLN_EPS) * g_ref[...] * (1.0 - lam_init)
        o_ref[0] = o.astype(o_ref.dtype)


def _flash_diff_attn(q, k, v, lam, g, li, *, cfg, lam_init):
    b, s, d = q.shape
    hd = cfg.head_dim
    hw = 2 * hd
    t = min(cfg.tq, s)
    nq = s // t
    pairs = [(qi, ki) for qi in range(nq) for ki in range(qi + 1)]
    qi_tab = jnp.asarray([p[0] for p in pairs], jnp.int32)
    ki_tab = jnp.asarray([p[1] for p in pairs], jnp.int32)
    body = functools.partial(_flash_body, hd=hd, lam_init=lam_init)
    grid_spec = pltpu.PrefetchScalarGridSpec(
        num_scalar_prefetch=2,
        grid=(b, d // hw, len(pairs)),
        in_specs=[pl.BlockSpec((1, t, hw), lambda bi, h, p, qt, kt: (bi, qt[p], h)),
                  pl.BlockSpec((1, t, hw), lambda bi, h, p, qt, kt: (bi, kt[p], h)),
                  pl.BlockSpec((1, t, hw), lambda bi, h, p, qt, kt: (bi, kt[p], h)),
                  pl.BlockSpec((None, 4, hd), lambda bi, h, p, qt, kt: (li, 0, 0)),
                  pl.BlockSpec((None, 1, hw), lambda bi, h, p, qt, kt: (li, 0, 0))],
        out_specs=pl.BlockSpec((1, t, hw), lambda bi, h, p, qt, kt: (bi, qt[p], h)),
        scratch_shapes=[pltpu.VMEM((2, t, 1), F32), pltpu.VMEM((2, t, 1), F32),
                        pltpu.VMEM((2, t, hw), F32)],
    )
    return pl.pallas_call(
        body,
        grid_spec=grid_spec,
        out_shape=jax.ShapeDtypeStruct((b, s, d), BF16),
        compiler_params=_params(("parallel", "parallel", "arbitrary"), 32 << 20),
        name="flash_diff_attn",
    )(qi_tab, ki_tab, q, k, v, lam, g)


def _decode_attn_body(pt_ref, q_ref, kn_ref, vn_ref, *rest, lam_init, n_steps, per_step):
    del pt_ref
    ck_refs, cv_refs = rest[:per_step], rest[per_step:2 * per_step]
    lam_ref, g_ref, o_ref, m_sc, l_sc, acc_sc = rest[2 * per_step:]
    step = pl.program_id(1)
    page, nh, hw = cv_refs[0].shape[2:]
    hd = hw // 2
    scale = hd ** -0.5

    def component(ref, m):
        return ref[0, pl.ds(m, nh, stride=2), :]

    @pl.when(step == 0)
    def _init():
        s0 = []
        for m in range(2):
            qb = component(q_ref, m).astype(BF16).astype(F32)
            kb = component(kn_ref, m).astype(BF16).astype(F32)
            s0.append(jnp.sum(qb * kb, axis=-1, keepdims=True) * scale)
        m_sc[...] = jnp.concatenate(s0, axis=0)
        l_sc[...] = jnp.ones(l_sc.shape, F32)
        vb = vn_ref[0].astype(BF16).astype(F32)
        acc_sc[...] = jnp.concatenate([vb, vb], axis=0)

    def page_scores(ck_ref):
        s = []
        for m in range(2):
            km = ck_ref[0, 0, :, pl.ds(m, nh, stride=2), :].reshape(page * nh, hd).astype(BF16)
            s.append(lax.dot_general(component(q_ref, m).astype(BF16), km,
                                     (((1,), (1,)), ((), ())), preferred_element_type=F32) * scale)
        s = jnp.concatenate(s, axis=0)
        row = lax.broadcasted_iota(jnp.int32, s.shape, 0)
        col = lax.broadcasted_iota(jnp.int32, s.shape, 1)
        return jnp.where((col & (nh - 1)) == (row & (nh - 1)), s, -jnp.inf)

    scores = [page_scores(ck_ref) for ck_ref in ck_refs]
    m_prev = m_sc[...]
    m_new = m_prev
    for s in scores:
        m_new = jnp.maximum(m_new, jnp.max(s, axis=-1, keepdims=True))
    alpha = jnp.exp(m_prev - m_new)
    l_new = alpha * l_sc[...]
    acc = alpha * acc_sc[...]
    for s, cv_ref in zip(scores, cv_refs):
        p = jnp.exp(s - m_new)
        l_new = l_new + jnp.sum(p, axis=-1, keepdims=True)
        v2 = cv_ref[0, 0].reshape(page * nh, hw).astype(BF16)
        acc = acc + jnp.dot(p.astype(BF16), v2, preferred_element_type=F32)
    l_sc[...] = l_new
    acc_sc[...] = acc
    m_sc[...] = m_new

    @pl.when(step == n_steps - 1)
    def _finish():
        lam = _diff_lambda(lam_ref[...], lam_init)
        a = acc_sc[...] / l_sc[...]
        o = a[0:nh] - lam * a[nh:2 * nh]
        o_ref[0] = _rms(o, SUBLN_EPS) * g_ref[...] * (1.0 - lam_init)


def _decode_diff_attn(q, k_new, v_new, cache_k, cache_v, page_table, lam, g, li, *, lam_init):
    bd, nc, hd = q.shape
    nh, hw = v_new.shape[1:]
    assert nh & (nh - 1) == 0, "head mask uses a power-of-two head count"
    page = cache_k.shape[2]
    n_pages = page_table.shape[1]
    per_step = 2 if n_pages % 2 == 0 else 1
    n_steps = n_pages // per_step
    body = functools.partial(_decode_attn_body, lam_init=lam_init, n_steps=n_steps,
                             per_step=per_step)

    def page_spec(shape, t):
        return pl.BlockSpec(shape, lambda b, p, pt: (pt[b * n_pages + p * per_step + t], li, 0, 0, 0))

    grid_spec = pltpu.PrefetchScalarGridSpec(
        num_scalar_prefetch=1,
        grid=(bd, n_steps),
        in_specs=[pl.BlockSpec((1, nc, hd), lambda b, p, pt: (b, 0, 0)),
                  pl.BlockSpec((1, nc, hd), lambda b, p, pt: (b, 0, 0)),
                  pl.BlockSpec((1, nh, hw), lambda b, p, pt: (b, 0, 0))]
        + [page_spec((1, 1, page, nc, hd), t) for t in range(per_step)]
        + [page_spec((1, 1, page, nh, hw), t) for t in range(per_step)]
        + [pl.BlockSpec((None, 4, hd), lambda b, p, pt: (li, 0, 0)),
           pl.BlockSpec((None, 1, hw), lambda b, p, pt: (li, 0, 0))],
        out_specs=pl.BlockSpec((1, nh, hw), lambda b, p, pt: (b, 0, 0)),
        scratch_shapes=[pltpu.VMEM((nc, 1), F32), pltpu.VMEM((nc, 1), F32),
                        pltpu.VMEM((nc, hw), F32)],
    )
    return pl.pallas_call(
        body,
        grid_spec=grid_spec,
        out_shape=jax.ShapeDtypeStruct((bd, nh, hw), F32),
        compiler_params=_params(("parallel", "arbitrary"),
                                per_step * 8 * _nbytes((page, nc, hd), F32) + (8 << 20)),
        name="decode_diff_attn",
    )(page_table.reshape(-1), q, k_new, v_new, *([cache_k] * per_step), *([cache_v] * per_step),
      lam, g)


def _ln_silu(c, g, b):
    mu = jnp.mean(c, axis=-1, keepdims=True)
    var = jnp.mean(jnp.square(c - mu), axis=-1, keepdims=True)
    y = (c - mu) * lax.rsqrt(var + LN_EPS) * g + b
    return jax.nn.silu(y)


def _conv_body(cur_ref, prev_ref, wdw_ref, bdw_ref, g_ref, b_ref, o_ref, ext_sc, c_sc, *,
               ts, halo, chunk):
    si = pl.program_id(1)
    ext_sc[0:halo, :] = jnp.where(si > 0, prev_ref[0], 0.0)
    ext_sc[halo:halo + ts, :] = cur_ref[0]
    ext_sc[halo + ts:, :] = jnp.zeros((V7X_SUBLANES, ext_sc.shape[1]), F32)
    lead = halo - (CONV_WIDTH - 1)
    d = cur_ref.shape[2]
    for c0 in range(0, d, chunk):
        cols = slice(c0, c0 + chunk)
        acc = None
        for r in range(V7X_SUBLANES):
            part = None
            for w in range(CONV_WIDTH):
                if (lead + w) % V7X_SUBLANES != r:
                    continue
                a8 = lead + w - r
                term = ext_sc[a8:a8 + ts + V7X_SUBLANES, cols] * wdw_ref[w:w + 1, cols]
                part = term if part is None else part + term
            if part is not None:
                shifted = part[r:r + ts]
                acc = shifted if acc is None else acc + shifted
        c_sc[:, cols] = acc + bdw_ref[:, cols]
    o_ref[0] = _ln_silu(c_sc[...], g_ref[...], b_ref[...]).astype(o_ref.dtype)


def _conv_prompt(u, w_dw, b_dw, ln_g, ln_b, li, *, cfg):
    b, s, d = u.shape
    ts = min(cfg.t_conv, s)
    halo = 32
    per = ts // halo
    body = functools.partial(_conv_body, ts=ts, halo=halo, chunk=min(d, 256))
    vec = pl.BlockSpec((None, 1, d), lambda bi, si: (li, 0, 0))
    return pl.pallas_call(
        body,
        grid=(b, s // ts),
        in_specs=[pl.BlockSpec((1, ts, d), lambda bi, si: (bi, si, 0)),
                  pl.BlockSpec((1, halo, d), lambda bi, si: (bi, jnp.maximum(si * per - 1, 0), 0)),
                  pl.BlockSpec((None, CONV_WIDTH, d), lambda bi, si: (li, 0, 0)),
                  vec, vec, vec],
        out_specs=pl.BlockSpec((1, ts, d), lambda bi, si: (bi, si, 0)),
        out_shape=jax.ShapeDtypeStruct((b, s, d), BF16),
        scratch_shapes=[pltpu.VMEM((halo + ts + V7X_SUBLANES, d), F32), pltpu.VMEM((ts, d), F32)],
        compiler_params=_params(("parallel", "arbitrary"), 32 << 20),
        name="conv_prompt",
    )(u, u, w_dw, b_dw, ln_g, ln_b)


def _conv_decode_body(hist_ref, u_ref, wdw_ref, bdw_ref, g_ref, b_ref, o_ref):
    nb = hist_ref.shape[0]
    k = CONV_WIDTH - 1
    w_hist = wdw_ref[0:k, :]
    rows = [jnp.sum(hist_ref[bi] * w_hist, axis=0, keepdims=True) for bi in range(nb)]
    c = jnp.concatenate(rows, axis=0) + u_ref[...] * wdw_ref[k:k + 1, :] + bdw_ref[...]
    o_ref[...] = _ln_silu(c, g_ref[...], b_ref[...]).astype(o_ref.dtype)


def _conv_decode(state_conv, u, w_dw, b_dw, ln_g, ln_b, li):
    bd, _, k, d = state_conv.shape
    vec = pl.BlockSpec((None, 1, d), lambda i: (li, 0, 0))
    return pl.pallas_call(
        _conv_decode_body,
        grid=(1,),
        in_specs=[pl.BlockSpec((bd, None, k, d), lambda i: (0, li, 0, 0)),
                  pl.BlockSpec((bd, d), lambda i: (0, 0)),
                  pl.BlockSpec((None, CONV_WIDTH, d), lambda i: (li, 0, 0)),
                  vec, vec, vec],
        out_specs=pl.BlockSpec((bd, d), lambda i: (0, 0)),
        out_shape=jax.ShapeDtypeStruct((bd, d), BF16),
        compiler_params=_params(("arbitrary",), 32 << 20),
        name="conv_decode",
    )(state_conv, u, w_dw, b_dw, ln_g, ln_b)


def _xattn_body(x_ref, g_ref, wq_ref, mk_ref, mv_ref, wo_ref, gn_ref, o_ref, hn_ref, *,
                n_heads, hd, pick_row):
    x = x_ref[...]
    h = (_rms(x, RMS_EPS) * g_ref[...]).astype(BF16)
    q = jnp.dot(h, wq_ref[...], preferred_element_type=F32).astype(BF16)
    mk = mk_ref[0, 0].astype(BF16)
    mv = mv_ref[0, 0].astype(BF16)
    scale = hd ** -0.5
    heads = []
    for hh in range(n_heads):
        sl = slice(hh * hd, (hh + 1) * hd)
        s = lax.dot_general(q[:, sl], mk[:, sl], (((1,), (1,)), ((), ())),
                            preferred_element_type=F32) * scale
        e = jnp.exp(s - jnp.max(s, axis=-1, keepdims=True))
        p = e / jnp.sum(e, axis=-1, keepdims=True)
        heads.append(jnp.dot(p.astype(BF16), mv[:, sl], preferred_element_type=F32))
    o = jnp.concatenate(heads, axis=1).astype(BF16)
    y = x + jnp.dot(o, wo_ref[...], preferred_element_type=F32)
    if pick_row:
        row = lax.broadcasted_iota(jnp.int32, y.shape, 0)
        y = jnp.sum(jnp.where(row == pl.program_id(0), y, 0.0), axis=0, keepdims=True)
    o_ref[...] = y
    hn_ref[...] = (_rms(y, RMS_EPS) * gn_ref[...]).astype(hn_ref.dtype)


def _cross_attn(x, g3, wq, mk, mv, wo, gn3, li, *, cfg, rows_per_mem):
    m, d = x.shape
    n_heads, hd = cfg.mem_heads, cfg.mem_head_dim
    e = n_heads * hd
    t_mem = mk.shape[2]
    pick_row = rows_per_mem == 0
    if pick_row:
        tm, steps = m, m
        x_spec = pl.BlockSpec((m, d), lambda i: (0, 0))
        mem_idx = lambda i: (i, li, 0, 0)
        out_spec = pl.BlockSpec((None, 1, d), lambda i: (i, 0, 0))
        out_dims = (m, 1, d)
    else:
        tm = min(cfg.t_xattn, rows_per_mem)
        steps = m // tm
        per = rows_per_mem // tm
        x_spec = pl.BlockSpec((tm, d), lambda i: (i, 0))
        mem_idx = lambda i: (i // per, li, 0, 0)
        out_spec = pl.BlockSpec((tm, d), lambda i: (i, 0))
        out_dims = (m, d)
    vec = lambda i: (li, 0, 0)
    body = functools.partial(_xattn_body, n_heads=n_heads, hd=hd, pick_row=pick_row)
    y, hn = pl.pallas_call(
        body,
        grid=(steps,),
        in_specs=[x_spec,
                  pl.BlockSpec((None, 1, d), vec),
                  pl.BlockSpec((None, d, e), vec),
                  pl.BlockSpec((1, 1, t_mem, e), mem_idx),
                  pl.BlockSpec((1, 1, t_mem, e), mem_idx),
                  pl.BlockSpec((None, e, d), vec),
                  pl.BlockSpec((None, 1, d), vec)],
        out_specs=[out_spec, out_spec],
        out_shape=[jax.ShapeDtypeStruct(out_dims, F32), jax.ShapeDtypeStruct(out_dims, BF16)],
        compiler_params=_params(("parallel",), 10 * _nbytes((tm, d), F32)
                                + 4 * _nbytes((d, e), BF16) + (8 << 20)),
        name="cross_attn",
    )(x, g3, wq, mk, mv, wo, gn3)
    return y.reshape(m, d), hn.reshape(m, d)


def _memkv_body(mem_ref, g_ref, wk_ref, wv_ref, ok_ref, ov_ref):
    h = (_rms(mem_ref[0], RMS_EPS) * g_ref[...]).astype(BF16)
    ok_ref[0, 0] = jnp.dot(h, wk_ref[...].astype(BF16), preferred_element_type=F32)
    ov_ref[0, 0] = jnp.dot(h, wv_ref[...].astype(BF16), preferred_element_type=F32)


def _mem_kv(mem, g3, w_kv, *, cfg):
    b, t, d = mem.shape
    n_layers = w_kv.shape[0]
    e = w_kv.shape[2] // 2
    tn = min(e, 256)
    nj = e // tn
    out = jax.ShapeDtypeStruct((b, n_layers, t, e), F32)
    out_spec = pl.BlockSpec((1, 1, t, tn), lambda l, j, bi: (bi, l, 0, j))
    return pl.pallas_call(
        _memkv_body,
        grid=(n_layers, nj, b),
        in_specs=[pl.BlockSpec((1, t, d), lambda l, j, bi: (bi, 0, 0)),
                  pl.BlockSpec((None, 1, d), lambda l, j, bi: (l, 0, 0)),
                  pl.BlockSpec((None, d, tn), lambda l, j, bi: (l, 0, j)),
                  pl.BlockSpec((None, d, tn), lambda l, j, bi: (l, 0, nj + j))],
        out_specs=[out_spec, out_spec],
        out_shape=[out, out],
        compiler_params=_params(("parallel", "parallel", "parallel"),
                                6 * _nbytes((d, tn), F32) + 4 * _nbytes((t, d), F32) + (8 << 20)),
        name="mem_kv",
    )(mem, g3, w_kv, w_kv)


def _rope_tables(pos, cfg):
    half = cfg.rot_dim // 2
    inv_freq = jnp.power(jnp.float32(ROPE_THETA),
                         -(jnp.arange(0, cfg.rot_dim, 2, dtype=F32) / cfg.rot_dim))
    ang = pos.astype(F32)[:, None] * inv_freq[None, :]
    cos, sin = jnp.cos(ang), jnp.sin(ang)
    pad = V7X_LANES - cfg.rot_dim
    n = pos.shape[0]
    c = jnp.concatenate([cos, cos, jnp.ones((n, pad), F32)], axis=1)
    s_lo = jnp.concatenate([-sin, jnp.zeros((n, half + pad), F32)], axis=1)
    s_hi = jnp.concatenate([jnp.zeros((n, half), F32), sin, jnp.zeros((n, pad), F32)], axis=1)
    return c, s_lo, s_hi


def _forward(cfg, x_prompt, x_sample, cache_k, cache_v, state_conv, cache_mem_k, cache_mem_v,
             page_table, mem_prompt, ln_ffn1, ffn1_w_in, ffn1_w_out, ln_mix, attn_w_qkv, attn_w_o,
             attn_lambda, attn_subln, conv_w_in, conv_b_in, conv_w_dw, conv_b_dw, conv_ln_g,
             conv_ln_b, conv_w_out, conv_b_out, ln_cross, ln_mem, cross_w_q, cross_w_kv, cross_w_o,
             ln_ffn2, ffn2_w_in, ffn2_w_out, ln_final):
    b, s, d = x_prompt.shape
    bd = x_sample.shape[0]
    depth = ln_ffn1.shape[0]
    d_ff = ffn1_w_out.shape[1]
    hd = cfg.head_dim
    hw = 2 * hd
    nc = d // hd
    e = cfg.mem_heads * cfg.mem_head_dim
    rot_half = cfg.rot_dim // 2

    row3 = lambda a: a.reshape(a.shape[0], 1, a.shape[-1])
    ln_ffn1, ln_mix, ln_cross, ln_mem, ln_ffn2 = map(row3, (ln_ffn1, ln_mix, ln_cross, ln_mem, ln_ffn2))
    conv_b_in, conv_b_dw, conv_ln_g, conv_ln_b, conv_b_out = map(
        row3, (conv_b_in, conv_b_dw, conv_ln_g, conv_ln_b, conv_b_out))
    subln = row3(attn_subln)
    wq_bf = cross_w_q.astype(BF16)
    wo_bf = cross_w_o.astype(BF16)
    mem_k_dec = cache_mem_k.reshape(bd, depth, -1, e)
    mem_v_dec = cache_mem_v.reshape(bd, depth, -1, e)

    rope_p = _rope_tables(jnp.arange(s, dtype=jnp.int32), cfg)
    rope_s = _rope_tables(jnp.full((bd,), cfg.past_len, jnp.int32), cfg)

    mem_k, mem_v = _mem_kv(mem_prompt, ln_mem, cross_w_kv, cfg=cfg)

    def ffn(xp, xs, hp, hs, w_in, w_out, li):
        ap, as_, w_out_bf = _mm(hp, hs, w_in, li, col_offsets=(0, d_ff), n_cols=d_ff,
                                tm=cfg.tm_ffn_in, tn=cfg.tn // 2, out_dtype=BF16,
                                out_dtype_s=BF16, act="swiglu", cast_next=w_out)
        return _mm(ap, as_, w_out_bf[None], 0, col_offsets=(0,), n_cols=d, tm=cfg.tm_ffn_out,
                   tn=cfg.tn_ffn_out, out_dtype=F32, out_dtype_s=F32, res=xp, res_s=xs,
                   res_scale=0.5, single_buffer_x=True)

    def proj(hp, hs, w, li, xp, xs, bias=None):
        return _mm(hp, hs, w, li, col_offsets=(0,), n_cols=d, tm=cfg.tm, tn=cfg.tn,
                   out_dtype=F32, out_dtype_s=F32, bias=bias, res=xp, res_s=xs, res_scale=1.0)

    xp = x_prompt.reshape(b * s, d)
    xs = x_sample.reshape(bd, d)
    kp_l, vp_l, ks_l, vs_l, cp_l, cs_l = [], [], [], [], [], []
    for i in range(depth):
        j = i // N_MIXERS
        hp, hs = _rmsnorm(xp, xs, ln_ffn1, i, BF16, cfg.t_norm)
        xp, xs = ffn(xp, xs, hp, hs, ffn1_w_in, ffn1_w_out, i)
        hp, hs = _rmsnorm(xp, xs, ln_mix, i, BF16, cfg.t_norm)
        if i % N_MIXERS == 0:
            lam_init = 0.8 - 0.6 * math.exp(-0.3 * i)
            mm = functools.partial(_mm, hp, hs, attn_w_qkv, j, n_cols=d, tm=cfg.tm, tn=cfg.tn,
                                   out_dtype_s=F32)
            rope = dict(rope=rope_p, rope_s=rope_s, rot_half=rot_half)
            qp, qs = mm(col_offsets=(0,), out_dtype=BF16, **rope)
            kp, ks = mm(col_offsets=(d,), out_dtype=F32, **rope)
            vp, vs = mm(col_offsets=(2 * d,), out_dtype=F32)
            op = _flash_diff_attn(qp.reshape(b, s, d), kp.reshape(b, s, d),
                                  vp.reshape(b, s, d), attn_lambda, subln, j,
                                  cfg=cfg, lam_init=lam_init)
            os_ = _decode_diff_attn(qs.reshape(bd, nc, hd), ks.reshape(bd, nc, hd),
                                    vs.reshape(bd, nc // 2, hw), cache_k, cache_v, page_table,
                                    attn_lambda, subln, j, lam_init=lam_init)
            xp, xs = proj(op.reshape(b * s, d), os_.reshape(bd, d).astype(BF16), attn_w_o, j,
                          xp, xs)
            kp_l.append(kp.reshape(b, s, nc, hd))
            vp_l.append(vp.reshape(b, s, nc // 2, hw))
            ks_l.append(ks.reshape(bd, 1, nc, hd))
            vs_l.append(vs.reshape(bd, 1, nc // 2, hw))
        else:
            up, us = _mm(hp, hs, conv_w_in, j, col_offsets=(0, d), n_cols=d, tm=cfg.tm,
                         tn=cfg.tn // 2, out_dtype=F32, out_dtype_s=F32, bias=conv_b_in, act="glu")
            up = up.reshape(b, s, d)
            cp = _conv_prompt(up, conv_w_dw, conv_b_dw, conv_ln_g, conv_ln_b, j, cfg=cfg)
            cs = _conv_decode(state_conv, us, conv_w_dw, conv_b_dw, conv_ln_g, conv_ln_b, j)
            xp, xs = proj(cp.reshape(b * s, d), cs, conv_w_out, j, xp, xs, bias=conv_b_out)
            k = CONV_WIDTH - 1
            cp_l.append(up[:, s - k:])
            cs_l.append(jnp.concatenate([state_conv[:, j, 1:], us[:, None, :]], axis=1))
        xp, hp = _cross_attn(xp, ln_cross, wq_bf, mem_k, mem_v, wo_bf, ln_ffn2, i, cfg=cfg,
                             rows_per_mem=s)
        xs, hs = _cross_attn(xs, ln_cross, wq_bf, mem_k_dec, mem_v_dec, wo_bf, ln_ffn2, i, cfg=cfg,
                             rows_per_mem=0)
        xp, xs = ffn(xp, xs, hp, hs, ffn2_w_in, ffn2_w_out, i)

    y_prompt, y_sample = _rmsnorm(xp, xs, ln_final.reshape(1, 1, d), 0, F32, cfg.t_norm)
    y_prompt = y_prompt.reshape(b, s, d)
    y_sample = y_sample.reshape(bd, 1, d)
    mh = (cfg.mem_heads, cfg.mem_head_dim)
    return (y_prompt, y_sample,
            jnp.stack(kp_l, axis=1), jnp.stack(vp_l, axis=1),
            jnp.stack(ks_l, axis=1), jnp.stack(vs_l, axis=1),
            jnp.stack(cp_l, axis=1), jnp.stack(cs_l, axis=1),
            mem_k.reshape(mem_k.shape[:3] + mh), mem_v.reshape(mem_v.shape[:3] + mh))


_CFG = Cfg(d_model=4096, head_dim=128, rot_dim=32, mem_heads=4, mem_head_dim=128,
           past_len=8192, tm=1024, tn=512, tm_ffn_in=1024, tm_ffn_out=1024, tn_ffn_out=256,
           tq=512, t_conv=64, t_xattn=256, t_norm=256)


def kernel(x_prompt, x_sample, cache_k, cache_v, state_conv, cache_mem_k, cache_mem_v, page_table, mem_prompt, ln_ffn1, ffn1_w_in, ffn1_w_out, ln_mix, attn_w_qkv, attn_w_o, attn_lambda, attn_subln, conv_w_in, conv_b_in, conv_w_dw, conv_b_dw, conv_ln_g, conv_ln_b, conv_w_out, conv_b_out, ln_cross, ln_mem, cross_w_q, cross_w_kv, cross_w_o, ln_ffn2, ffn2_w_in, ffn2_w_out, ln_final):
    return _forward(_CFG, x_prompt, x_sample, cache_k, cache_v, state_conv, cache_mem_k, cache_mem_v,
                    page_table, mem_prompt, ln_ffn1, ffn1_w_in, ffn1_w_out, ln_mix, attn_w_qkv,
                    attn_w_o, attn_lambda, attn_subln, conv_w_in, conv_b_in, conv_w_dw, conv_b_dw,
                    conv_ln_g, conv_ln_b, conv_w_out, conv_b_out, ln_cross, ln_mem, cross_w_q,
                    cross_w_kv, cross_w_o, ln_ffn2, ffn2_w_in, ffn2_w_out, ln_final)
```
